```python
import jax, jax.numpy as jnp
from jax import lax
import numpy as np

D_MODEL = 1024
BATCH = 32
SEQ = 2048
DEPTH = 2

N_MIXERS = 2
N_ATTN_LAYERS = (DEPTH + 1) // 2
N_REC_LAYERS = DEPTH // 2

HEAD_DIM = 64
N_HEADS = D_MODEL // HEAD_DIM
N_KV_HEADS = 2
GROUP = N_HEADS // N_KV_HEADS
ATTN_WIDTH = N_HEADS * HEAD_DIM
KV_WIDTH = N_KV_HEADS * HEAD_DIM
ATTN_IN = 2 * ATTN_WIDTH + 2 * KV_WIDTH
WINDOW = 128
ATTN_BLOCK = 128
ROPE_THETA = 500000.0
ROPE_DIM = HEAD_DIM // 4

REC_HEADS = 8
REC_KEY_DIM = 128
REC_VALUE_DIM = D_MODEL // REC_HEADS
FORGET_DIM = REC_HEADS * REC_KEY_DIM
REC_WIDTH = REC_HEADS * REC_VALUE_DIM
REC_IN = 2 * FORGET_DIM + 2 * REC_WIDTH
REC_CHUNK = 32

NORM_EPS = 1e-6

kernel_name = "hybrid_swa_sink_hgrn2_interleaved"


def rmsnorm(x, w):
    xf = x.astype(jnp.float32)
    y = xf * lax.rsqrt(jnp.mean(xf * xf, axis=-1, keepdims=True) + NORM_EPS)
    return (y * w.astype(jnp.float32)).astype(x.dtype)


def partial_rope(x, positions):
    half = ROPE_DIM // 2
    inv_freq = ROPE_THETA ** (-(jnp.arange(half, dtype=jnp.float32) * 2.0 / ROPE_DIM))
    ang = positions.astype(jnp.float32)[..., None] * inv_freq
    cos = jnp.cos(ang)[:, :, None, :]
    sin = jnp.sin(ang)[:, :, None, :]
    x1 = x[..., :half].astype(jnp.float32)
    x2 = x[..., half:ROPE_DIM].astype(jnp.float32)
    r1 = x1 * cos - x2 * sin
    r2 = x2 * cos + x1 * sin
    return jnp.concatenate([r1.astype(x.dtype), r2.astype(x.dtype), x[..., ROPE_DIM:]], axis=-1)


def sliding_window_gqa(q, k, v, sinks):
    B, T = q.shape[0], q.shape[1]
    nb = T // ATTN_BLOCK
    qb = q.reshape(B, nb, ATTN_BLOCK, N_KV_HEADS, GROUP, HEAD_DIM).transpose(1, 0, 2, 3, 4, 5)

    def span(t):
        tb = t.reshape(B, nb, ATTN_BLOCK, N_KV_HEADS, HEAD_DIM)
        prev = jnp.pad(tb, ((0, 0), (1, 0), (0, 0), (0, 0), (0, 0)))[:, :-1]
        return jnp.concatenate([prev, tb], axis=2).transpose(1, 0, 2, 3, 4)

    kk, vv = span(k), span(v)
    q_rel = jnp.arange(ATTN_BLOCK)[:, None] + ATTN_BLOCK
    k_rel = jnp.arange(2 * ATTN_BLOCK)[None, :]
    band = (k_rel <= q_rel) & (q_rel - k_rel < WINDOW)
    sink = sinks.astype(jnp.float32).reshape(N_KV_HEADS, GROUP)[None, :, :, None]
    scale = HEAD_DIM ** -0.5

    def block(args):
        qi, ki, vi, idx = args
        s = jnp.einsum('bqhgd,bkhd->bhgqk', qi, ki, preferred_element_type=jnp.float32) * scale
        valid = band & ((idx > 0) | (k_rel >= ATTN_BLOCK))
        s = jnp.where(valid, s, -jnp.inf)
        m = jnp.maximum(jnp.max(s, axis=-1), sink)
        p = jnp.exp(s - m[..., None])
        denom = jnp.sum(p, axis=-1) + jnp.exp(sink - m)
        p = (p / denom[..., None]).astype(vi.dtype)
        return jnp.einsum('bhgqk,bkhd->bqhgd', p, vi)

    out = lax.map(block, (qb, kk, vv, jnp.arange(nb)))
    return out.transpose(1, 0, 2, 3, 4, 5).reshape(B, T, ATTN_WIDTH)


def attention_mixer(h, positions, w_in, b_in, sinks, w_out, b_out):
    B, T = h.shape[0], h.shape[1]
    proj = h @ w_in + b_in
    q, k, v, z = jnp.split(proj, [ATTN_WIDTH, ATTN_WIDTH + KV_WIDTH, ATTN_WIDTH + 2 * KV_WIDTH], axis=-1)
    q = partial_rope(q.reshape(B, T, N_HEADS, HEAD_DIM), positions)
    k = partial_rope(k.reshape(B, T, N_KV_HEADS, HEAD_DIM), positions)
    v = v.reshape(B, T, N_KV_HEADS, HEAD_DIM)
    o = sliding_window_gqa(q, k, v, sinks)
    return (o * jax.nn.silu(z)) @ w_out + b_out


def chunked_gated_recurrence(q, k, v, log_f):
    B, T, H, K = q.shape
    V = v.shape[-1]
    nc = T // REC_CHUNK

    def to_chunks(t):
        return t.reshape(B, nc, REC_CHUNK, H, t.shape[-1]).transpose(1, 0, 3, 2, 4)

    qc, kc, vc, gc = to_chunks(q), to_chunks(k), to_chunks(v), to_chunks(log_f)
    causal = jnp.tril(jnp.ones((REC_CHUNK, REC_CHUNK), dtype=bool))[:, :, None]

    def step(S, inp):
        qi, ki, vi, gi = inp
        b = jnp.cumsum(gi, axis=2)
        b_last = b[:, :, -1:, :]
        o_inter = jnp.einsum('bhck,bhkv->bhcv', qi * jnp.exp(b), S)
        diff = b[:, :, :, None, :] - b[:, :, None, :, :]
        decay = jnp.exp(jnp.where(causal, diff, -jnp.inf))
        scores = jnp.einsum('bhtk,bhtsk->bhts', qi, decay * ki[:, :, None, :, :])
        o_intra = jnp.einsum('bhts,bhsv->bhtv', scores, vi)
        S_new = S * jnp.exp(b_last)[:, :, 0, :, None] + jnp.einsum(
            'bhck,bhcv->bhkv', ki * jnp.exp(b_last - b), vi)
        return S_new, o_inter + o_intra

    S0 = jnp.zeros((B, H, K, V), jnp.float32)
    _, o = lax.scan(step, S0, (qc, kc, vc, gc))
    return o.transpose(1, 0, 3, 2, 4).reshape(B, T, H, V)


def hgrn2_mixer(h, lower_bound, w_in, gnorm_w, w_out):
    B, T = h.shape[0], h.shape[1]
    proj = h @ w_in
    q, f, i, z = jnp.split(proj, [FORGET_DIM, 2 * FORGET_DIM, 2 * FORGET_DIM + REC_WIDTH], axis=-1)
    q = jax.nn.silu(q.astype(jnp.float32)).reshape(B, T, REC_HEADS, REC_KEY_DIM)
    lb = lower_bound.astype(jnp.float32)
    log_f = jnp.logaddexp(jnp.log(lb), jnp.log1p(-lb) + jax.nn.log_sigmoid(f.astype(jnp.float32)))
    k = -jnp.expm1(log_f)
    log_f = log_f.reshape(B, T, REC_HEADS, REC_KEY_DIM)
    k = k.reshape(B, T, REC_HEADS, REC_KEY_DIM)
    v = i.astype(jnp.float32).reshape(B, T, REC_HEADS, REC_VALUE_DIM)
    o = chunked_gated_recurrence(q, k, v, log_f)
    o = o * lax.rsqrt(jnp.mean(o * o, axis=-1, keepdims=True) + NORM_EPS) * gnorm_w.astype(jnp.float32)
    o = o.reshape(B, T, REC_WIDTH) * jax.nn.silu(z.astype(jnp.float32))
    return o.astype(h.dtype) @ w_out


def setup_inputs(seed: int = 0) -> dict:
    key = jax.random.key(seed)
    ks = jax.random.split(key, 16)
    f32 = jnp.float32
    x = jax.random.normal(ks[0], (BATCH, SEQ, D_MODEL), f32)
    offsets = jax.random.randint(ks[1], (BATCH, 1), 0, 4096, dtype=jnp.int32)
    positions = offsets + jnp.arange(SEQ, dtype=jnp.int32)[None, :]
    pre_norm_w = 1.0 + 0.02 * jax.random.normal(ks[2], (DEPTH, D_MODEL), f32)
    post_norm_w = 1.0 + 0.02 * jax.random.normal(ks[3], (DEPTH, D_MODEL), f32)
    attn_w_in = jax.random.normal(ks[4], (N_ATTN_LAYERS, D_MODEL, ATTN_IN), f32) * D_MODEL ** -0.5
    attn_b_in = 0.02 * jax.random.normal(ks[5], (N_ATTN_LAYERS, ATTN_IN), f32)
    attn_sinks = 0.5 * jax.random.normal(ks[6], (N_ATTN_LAYERS, N_HEADS), f32)
    attn_w_out = jax.random.normal(ks[7], (N_ATTN_LAYERS, ATTN_WIDTH, D_MODEL), f32) * ATTN_WIDTH ** -0.5
    attn_b_out = 0.02 * jax.random.normal(ks[8], (N_ATTN_LAYERS, D_MODEL), f32)
    rec_w_in = jax.random.normal(ks[9], (N_REC_LAYERS, D_MODEL, REC_IN), f32) * D_MODEL ** -0.5
    rec_lb_logits = 0.5 * jax.random.normal(ks[10], (DEPTH, FORGET_DIM), f32)
    rec_gnorm_w = 1.0 + 0.02 * jax.random.normal(ks[11], (N_REC_LAYERS, REC_VALUE_DIM), f32)
    rec_w_out = jax.random.normal(ks[12], (N_REC_LAYERS, REC_WIDTH, D_MODEL), f32) * REC_WIDTH ** -0.5
    return {"x": x, "positions": positions, "pre_norm_w": pre_norm_w, "post_norm_w": post_norm_w,
            "attn_w_in": attn_w_in, "attn_b_in": attn_b_in, "attn_sinks": attn_sinks,
            "attn_w_out": attn_w_out, "attn_b_out": attn_b_out, "rec_w_in": rec_w_in,
            "rec_lb_logits": rec_lb_logits, "rec_gnorm_w": rec_gnorm_w, "rec_w_out": rec_w_out}


def reference(x, positions, pre_norm_w, post_norm_w, attn_w_in, attn_b_in, attn_sinks,
              attn_w_out, attn_b_out, rec_w_in, rec_lb_logits, rec_gnorm_w, rec_w_out):
    probs = jax.nn.softmax(rec_lb_logits.astype(jnp.float32), axis=0)
    cum = jnp.cumsum(probs, axis=0)
    lower_bounds = cum - cum[0:1]
    for layer in range(DEPTH):
        h = rmsnorm(x, pre_norm_w[layer])
        j = layer // N_MIXERS
        if layer % N_MIXERS == 0:
            y = attention_mixer(h, positions, attn_w_in[j], attn_b_in[j], attn_sinks[j],
                                attn_w_out[j], attn_b_out[j])
        else:
            y = hgrn2_mixer(h, lower_bounds[layer], rec_w_in[j], rec_gnorm_w[j], rec_w_out[j])
        x = x + rmsnorm(y, post_norm_w[layer])
    return x
```

```python
import functools

import numpy as np
import jax
import jax.numpy as jnp
from jax import lax
from jax.experimental import pallas as pl
from jax.experimental.pallas import tpu as pltpu

D_MODEL = 1024
N_MIXERS = 2

HEAD_DIM = 64
N_HEADS = D_MODEL // HEAD_DIM
N_KV_HEADS = 2
GROUP = N_HEADS // N_KV_HEADS
ATTN_WIDTH = N_HEADS * HEAD_DIM
KV_WIDTH = N_KV_HEADS * HEAD_DIM
ATTN_IN = 2 * ATTN_WIDTH + 2 * KV_WIDTH
ATTN_BLOCK = 128
ROPE_THETA = 500000.0
ROPE_DIM = HEAD_DIM // 4
ROPE_HALF = ROPE_DIM // 2

REC_HEADS = 8
REC_KEY_DIM = 128
REC_VALUE_DIM = D_MODEL // REC_HEADS
FORGET_DIM = REC_HEADS * REC_KEY_DIM
REC_WIDTH = REC_HEADS * REC_VALUE_DIM
REC_IN = 2 * FORGET_DIM + 2 * REC_WIDTH

NORM_EPS = 1e-6

LANES = 128
SUBLANES = 8
ATTN_TILE = 256
REC_CHUNK = 256
VMEM_LIMIT_BYTES = 48 * 1024 * 1024

F32 = jnp.float32
BF16 = jnp.bfloat16


def _rmsnorm(x, w):
    return x * lax.rsqrt(jnp.mean(x * x, axis=-1, keepdims=True) + NORM_EPS) * w


def _sigmoid(x):
    return 1.0 / (1.0 + jnp.exp(-x))


def _dot(a, b):
    return jnp.dot(a, b, preferred_element_type=F32)


def _dot_nt(a, b):
    return lax.dot_general(a, b, (((1,), (1,)), ((), ())), preferred_element_type=F32)


def _dot_tn(a, b):
    return lax.dot_general(a, b, (((0,), (0,)), ((), ())), preferred_element_type=F32)


def _attn_layer_kernel(sinks_ref, x_ref, pos_ref, invf_ref, prew_ref, postw_ref, win_ref, bin_ref,
                       wout_ref, bout_ref, o_ref, kprev_ref, vprev_ref, proj_ref, att_ref):
    t = pl.program_id(1)
    tq = x_ref.shape[0]
    n_blocks = tq // ATTN_BLOCK

    @pl.when(t == 0)
    def _():
        kprev_ref[...] = jnp.zeros_like(kprev_ref)
        vprev_ref[...] = jnp.zeros_like(vprev_ref)

    x = x_ref[...]
    h = _rmsnorm(x, prew_ref[...]).astype(BF16)
    proj_ref[...] = _dot(h, win_ref[...]) + bin_ref[...]

    ang = pos_ref[...].astype(F32) * invf_ref[...]
    lane = lax.broadcasted_iota(jnp.int32, (1, LANES), 1)
    d = lane % HEAD_DIM
    cos, sin = jnp.cos(ang), jnp.sin(ang)
    c_tab = jnp.where(d < ROPE_DIM, cos, 1.0)
    s_lo = jnp.where(d < ROPE_HALF, -sin, 0.0)
    s_hi = jnp.where((d >= ROPE_HALF) & (d < ROPE_DIM), sin, 0.0)

    def rope(xc, rows=slice(None)):
        return (xc * c_tab[rows] + pltpu.roll(xc, LANES - ROPE_HALF, 1) * s_lo[rows]
                + pltpu.roll(xc, ROPE_HALF, 1) * s_hi[rows])

    k_all = rope(proj_ref[:, ATTN_WIDTH:ATTN_WIDTH + KV_WIDTH])
    v_all = proj_ref[:, ATTN_WIDTH + KV_WIDTH:ATTN_WIDTH + 2 * KV_WIDTH]

    lo_half = lane < HEAD_DIM
    row = lax.broadcasted_iota(jnp.int32, (ATTN_BLOCK, 4 * ATTN_BLOCK), 0)
    col = lax.broadcasted_iota(jnp.int32, (ATTN_BLOCK, 4 * ATTN_BLOCK), 1) % (2 * ATTN_BLOCK)
    band = (col > row) & (col <= row + ATTN_BLOCK)
    scale = HEAD_DIM ** -0.5

    def block_diag(a):
        a_sw = pltpu.roll(a, HEAD_DIM, 1)
        zero = jnp.zeros_like(a)
        g0 = jnp.concatenate([jnp.where(lo_half, a, zero), jnp.where(lo_half, zero, a_sw)], axis=0)
        g1 = jnp.concatenate([jnp.where(lo_half, a_sw, zero), jnp.where(lo_half, zero, a)], axis=0)
        return g0.astype(BF16), g1.astype(BF16)

    k_prev = kprev_ref[...]
    v_prev = vprev_ref[...]
    for j in range(n_blocks):
        rows = slice(j * ATTN_BLOCK, (j + 1) * ATTN_BLOCK)
        k_cur, v_cur = k_all[rows], v_all[rows]
        kb = block_diag(jnp.concatenate([k_prev, k_cur], axis=0))
        vb = block_diag(jnp.concatenate([v_prev, v_cur], axis=0))
        if j == 0:
            first_key = jnp.where(t == 0, ATTN_BLOCK, 0)
            bias = jnp.where(band & (col >= first_key), 0.0, -jnp.inf)
        else:
            bias = jnp.where(band, 0.0, -jnp.inf)
        for p in range(N_HEADS // 2):
            g = (2 * p) // GROUP
            lanes = slice(p * LANES, (p + 1) * LANES)
            qp = (rope(proj_ref[rows, lanes], rows) * scale).astype(BF16)
            s = _dot_nt(qp, kb[g]) + bias
            halves = []
            invs = []
            for e in range(2):
                se = s[:, e * 2 * ATTN_BLOCK:(e + 1) * 2 * ATTN_BLOCK]
                sink = sinks_ref[2 * p + e]
                m = jnp.maximum(jnp.max(se, axis=-1, keepdims=True), sink)
                pe = jnp.exp(se - m)
                denom = jnp.sum(pe, axis=-1, keepdims=True) + jnp.exp(sink - m)
                halves.append(pe.astype(BF16))
                invs.append(1.0 / denom)
            o = _dot(jnp.concatenate(halves, axis=1), vb[g])
            att_ref[rows, lanes] = o * jnp.where(lo_half, invs[0], invs[1])
        k_prev, v_prev = k_cur, v_cur
    kprev_ref[...] = k_prev
    vprev_ref[...] = v_prev

    z = proj_ref[:, ATTN_WIDTH + 2 * KV_WIDTH:]
    gated = (att_ref[...] * (z * _sigmoid(z))).astype(BF16)
    y = _dot(gated, wout_ref[...]) + bout_ref[...]
    o_ref[...] = x + _rmsnorm(y, postw_ref[...])


def _attn_layer(x, positions, pre_w, post_w, w_in, b_in, sinks, w_out, b_out):
    B, T, D = x.shape
    tq = min(ATTN_TILE, T)
    assert T % tq == 0 and tq % ATTN_BLOCK == 0 and D == D_MODEL
    inv_freq = ROPE_THETA ** (-(jnp.arange(ROPE_HALF, dtype=F32) * 2.0 / ROPE_DIM))
    invf = jnp.tile(inv_freq, LANES // ROPE_HALF).reshape(1, LANES)
    const = lambda b, t: (0, 0)
    return pl.pallas_call(
        _attn_layer_kernel,
        out_shape=jax.ShapeDtypeStruct((B, T, D), x.dtype),
        grid=(B, T // tq),
        in_specs=[
            pl.BlockSpec(memory_space=pltpu.SMEM),
            pl.BlockSpec((None, tq, D), lambda b, t: (b, t, 0)),
            pl.BlockSpec((None, tq, 1), lambda b, t: (b, t, 0)),
            pl.BlockSpec((1, LANES), const),
            pl.BlockSpec((1, D), const),
            pl.BlockSpec((1, D), const),
            pl.BlockSpec((D, ATTN_IN), const),
            pl.BlockSpec((1, ATTN_IN), const),
            pl.BlockSpec((ATTN_WIDTH, D), const),
            pl.BlockSpec((1, D), const),
        ],
        out_specs=pl.BlockSpec((None, tq, D), lambda b, t: (b, t, 0)),
        scratch_shapes=[
            pltpu.VMEM((ATTN_BLOCK, KV_WIDTH), F32),
            pltpu.VMEM((ATTN_BLOCK, KV_WIDTH), F32),
            pltpu.VMEM((tq, ATTN_IN), F32),
            pltpu.VMEM((tq, ATTN_WIDTH), F32),
        ],
        compiler_params=pltpu.CompilerParams(
            dimension_semantics=("arbitrary", "arbitrary"),
            vmem_limit_bytes=VMEM_LIMIT_BYTES),
        name="swa_layer",
    )(sinks.astype(F32), x, positions.reshape(B, T, 1), invf, pre_w.reshape(1, D), post_w.reshape(1, D),
      w_in.astype(BF16), b_in.reshape(1, ATTN_IN), w_out.astype(BF16), b_out.reshape(1, D))


def _boundary_rows(b, m):
    c, n = b.shape
    if 2 * m >= SUBLANES:
        b3 = b.reshape(c // (2 * m), 2 * m, n)
        return jnp.broadcast_to(b3[:, m - 1:m, :], b3.shape).reshape(c, n)
    b3 = b.reshape(c // SUBLANES, SUBLANES, n)
    sub = lax.broadcasted_iota(jnp.int32, b3.shape, 1)
    r = jnp.broadcast_to(b3[:, SUBLANES - m - 1:SUBLANES - m, :], b3.shape)
    for start in range(SUBLANES - 4 * m, -1, -2 * m):
        r = jnp.where(sub < start + 2 * m, jnp.broadcast_to(b3[:, start + m - 1:start + m, :], b3.shape), r)
    return r.reshape(c, n)


def _rec_layer_kernel(layer, x_ref, lbl_ref, lv_ref, ltri_ref, prew_ref, postw_ref, win_ref, gnw_ref,
                      wout_ref, o_ref, state_ref, proj_ref, gated_ref):
    c = pl.program_id(1)
    chunk = x_ref.shape[0]
    n_levels = chunk.bit_length() - 1

    @pl.when(c == 0)
    def _():
        state_ref[...] = jnp.zeros_like(state_ref)

    x = x_ref[...]
    h = _rmsnorm(x, prew_ref[...]).astype(BF16)
    proj_ref[...] = _dot(h, win_ref[...])

    def head(hd, carry):
        off = pl.multiple_of(hd * LANES, LANES)
        q_raw = proj_ref[:, pl.ds(off, LANES)]
        f_raw = proj_ref[:, pl.ds(FORGET_DIM + off, LANES)]
        v = proj_ref[:, pl.ds(2 * FORGET_DIM + off, LANES)]
        z = proj_ref[:, pl.ds(2 * FORGET_DIM + REC_WIDTH + off, LANES)]

        logits = lbl_ref[:, pl.ds(off, LANES)]
        ex = jnp.exp(logits - jnp.max(logits, axis=0, keepdims=True))
        lb = jnp.sum(ex[1:layer + 1], axis=0, keepdims=True) / jnp.sum(ex, axis=0, keepdims=True)

        e = jnp.exp(-jnp.abs(f_raw))
        sp = 1.0 / (1.0 + e)
        pos = f_raw >= 0
        sig = jnp.where(pos, sp, e * sp)
        nsig = jnp.where(pos, e * sp, sp)
        log_f = jnp.log(lb + (1.0 - lb) * sig)
        k = (1.0 - lb) * nsig
        q = q_raw * _sigmoid(q_raw)

        g_hi = log_f.astype(BF16)
        g_lo = (log_f - g_hi.astype(F32)).astype(BF16)
        ltri = ltri_ref[...]
        b = _dot(ltri, g_hi) + _dot(ltri, g_lo)

        lv = lv_ref[...]
        a = jnp.where(lv == -1, _dot_nt(q.astype(BF16), k.astype(BF16)), 0.0)
        for j in range(n_levels):
            en = jnp.exp(-jnp.abs(b - _boundary_rows(b, 1 << j)))
            a = jnp.where(lv == j, _dot_nt((q * en).astype(BF16), (k * en).astype(BF16)), a)
        o = _dot(a.astype(BF16), v.astype(BF16))

        b_last = b[chunk - 1:chunk, :]
        st = state_ref[hd]
        o = o + _dot_nt((q * jnp.exp(b)).astype(BF16), st.astype(BF16))
        kd = (k * jnp.exp(b_last - b)).astype(BF16)
        state_ref[hd] = st * jnp.exp(b_last) + _dot_tn(v.astype(BF16), kd)

        o = o * lax.rsqrt(jnp.mean(o * o, axis=-1, keepdims=True) + NORM_EPS) * gnw_ref[...]
        gated_ref[:, pl.ds(off, LANES)] = (o * (z * _sigmoid(z))).astype(BF16)
        return carry

    lax.fori_loop(0, REC_HEADS, head, 0)

    y = _dot(gated_ref[...], wout_ref[...])
    o_ref[...] = x + _rmsnorm(y, postw_ref[...])


def _level_index(c):
    t = np.arange(c)[:, None]
    s = np.arange(c)[None, :]
    x = np.maximum(t ^ s, 1)
    lv = np.floor(np.log2(x)).astype(np.int32)
    lv = np.where(s < t, lv, np.where(s == t, -1, 99))
    return jnp.asarray(lv, dtype=jnp.int32)


def _rec_layer(x, layer, lb_logits, pre_w, post_w, w_in, gnorm_w, w_out):
    B, T, D = x.shape
    chunk = min(REC_CHUNK, T)
    assert T % chunk == 0 and chunk & (chunk - 1) == 0 and chunk >= 2 * SUBLANES and D == D_MODEL
    depth = lb_logits.shape[0]
    ltri = jnp.asarray(np.tril(np.ones((chunk, chunk), np.float32)), dtype=BF16)
    const = lambda b, t: (0, 0)
    return pl.pallas_call(
        functools.partial(_rec_layer_kernel, layer),
        out_shape=jax.ShapeDtypeStruct((B, T, D), x.dtype),
        grid=(B, T // chunk),
        in_specs=[
            pl.BlockSpec((None, chunk, D), lambda b, t: (b, t, 0)),
            pl.BlockSpec((depth, FORGET_DIM), const),
            pl.BlockSpec((chunk, chunk), const),
            pl.BlockSpec((chunk, chunk), const),
            pl.BlockSpec((1, D), const),
            pl.BlockSpec((1, D), const),
            pl.BlockSpec((D, REC_IN), const),
            pl.BlockSpec((1, REC_VALUE_DIM), const),
            pl.BlockSpec((REC_WIDTH, D), const),
        ],
        out_specs=pl.BlockSpec((None, chunk, D), lambda b, t: (b, t, 0)),
        scratch_shapes=[
            pltpu.VMEM((REC_HEADS, REC_VALUE_DIM, REC_KEY_DIM), F32),
            pltpu.VMEM((chunk, REC_IN), F32),
            pltpu.VMEM((chunk, REC_WIDTH), BF16),
        ],
        compiler_params=pltpu.CompilerParams(
            dimension_semantics=("arbitrary", "arbitrary"),
            vmem_limit_bytes=VMEM_LIMIT_BYTES),
        name="hgrn2_layer",
    )(x, lb_logits.astype(F32), _level_index(chunk), ltri, pre_w.reshape(1, D), post_w.reshape(1, D),
      w_in.astype(BF16), gnorm_w.reshape(1, REC_VALUE_DIM), w_out.astype(BF16))


def kernel(x, positions, pre_norm_w, post_norm_w, attn_w_in, attn_b_in, attn_sinks, attn_w_out, attn_b_out,
           rec_w_in, rec_lb_logits, rec_gnorm_w, rec_w_out):
    depth = pre_norm_w.shape[0]
    for layer in range(depth):
        j = layer // N_MIXERS
        if layer % N_MIXERS == 0:
            x = _attn_layer(x, positions, pre_norm_w[layer], post_norm_w[layer], attn_w_in[j], attn_b_in[j],
                            attn_sinks[j], attn_w_out[j], attn_b_out[j])
        else:
            x = _rec_layer(x, layer, rec_lb_logits, pre_norm_w[layer], post_norm_w[layer], rec_w_in[j],
                           rec_gnorm_w[j], rec_w_out[j])
    return x
```

```python
import functools

import numpy as np
import jax
import jax.numpy as jnp
from jax import lax
from jax.experimental import pallas as pl
from jax.experimental.pallas import tpu as pltpu

D_MODEL = 1024
N_MIXERS = 2

HEAD_DIM = 64
N_HEADS = D_MODEL // HEAD_DIM
N_KV_HEADS = 2
GROUP = N_HEADS // N_KV_HEADS
ATTN_WIDTH = N_HEADS * HEAD_DIM
KV_WIDTH = N_KV_HEADS * HEAD_DIM
ATTN_IN = 2 * ATTN_WIDTH + 2 * KV_WIDTH
ATTN_BLOCK = 128
ROPE_THETA = 500000.0
ROPE_DIM = HEAD_DIM // 4
ROPE_HALF = ROPE_DIM // 2

REC_HEADS = 8
REC_KEY_DIM = 128
REC_VALUE_DIM = D_MODEL // REC_HEADS
FORGET_DIM = REC_HEADS * REC_KEY_DIM
REC_WIDTH = REC_HEADS * REC_VALUE_DIM
REC_IN = 2 * FORGET_DIM + 2 * REC_WIDTH

NORM_EPS = 1e-6

LANES = 128
SUBLANES = 8
ATTN_TILE = 256
REC_CHUNK = 256
REC_SEG = 64
VMEM_LIMIT_BYTES = 48 * 1024 * 1024

F32 = jnp.float32
BF16 = jnp.bfloat16


def _rmsnorm(x, w):
    return x * lax.rsqrt(jnp.mean(x * x, axis=-1, keepdims=True) + NORM_EPS) * w


def _sigmoid(x):
    return 1.0 / (1.0 + jnp.exp(-x))


def _dot(a, b):
    return jnp.dot(a, b, preferred_element_type=F32)


def _dot_nt(a, b):
    return lax.dot_general(a, b, (((1,), (1,)), ((), ())), preferred_element_type=F32)


def _dot_tn(a, b):
    return lax.dot_general(a, b, (((0,), (0,)), ((), ())), preferred_element_type=F32)


def _attn_layer_kernel(sinks_ref, x_ref, pos_ref, invf_ref, prew_ref, postw_ref, win_ref, bin_ref,
                       wout_ref, bout_ref, o_ref, kprev_ref, vprev_ref, proj_ref, att_ref):
    t = pl.program_id(1)
    tq = x_ref.shape[0]
    n_blocks = tq // ATTN_BLOCK

    @pl.when(t == 0)
    def _():
        kprev_ref[...] = jnp.zeros_like(kprev_ref)
        vprev_ref[...] = jnp.zeros_like(vprev_ref)

    x = x_ref[...]
    h = _rmsnorm(x, prew_ref[...]).astype(BF16)
    proj_ref[...] = _dot(h, win_ref[...]) + bin_ref[...]

    ang = pos_ref[...].astype(F32) * invf_ref[...]
    lane = lax.broadcasted_iota(jnp.int32, (1, LANES), 1)
    d = lane % HEAD_DIM
    cos, sin = jnp.cos(ang), jnp.sin(ang)
    c_tab = jnp.where(d < ROPE_DIM, cos, 1.0)
    s_lo = jnp.where(d < ROPE_HALF, -sin, 0.0)
    s_hi = jnp.where((d >= ROPE_HALF) & (d < ROPE_DIM), sin, 0.0)

    def rope(xc, rows=slice(None)):
        return (xc * c_tab[rows] + pltpu.roll(xc, LANES - ROPE_HALF, 1) * s_lo[rows]
                + pltpu.roll(xc, ROPE_HALF, 1) * s_hi[rows])

    k_all = rope(proj_ref[:, ATTN_WIDTH:ATTN_WIDTH + KV_WIDTH])
    v_all = proj_ref[:, ATTN_WIDTH + KV_WIDTH:ATTN_WIDTH + 2 * KV_WIDTH]

    lo_half = lane < HEAD_DIM
    row = lax.broadcasted_iota(jnp.int32, (ATTN_BLOCK, 4 * ATTN_BLOCK), 0)
    col = lax.broadcasted_iota(jnp.int32, (ATTN_BLOCK, 4 * ATTN_BLOCK), 1) % (2 * ATTN_BLOCK)
    band = (col > row) & (col <= row + ATTN_BLOCK)
    scale = HEAD_DIM ** -0.5

    def block_diag(a):
        a_sw = pltpu.roll(a, HEAD_DIM, 1)
        zero = jnp.zeros_like(a)
        g0 = jnp.concatenate([jnp.where(lo_half, a, zero), jnp.where(lo_half, zero, a_sw)], axis=0)
        g1 = jnp.concatenate([jnp.where(lo_half, a_sw, zero), jnp.where(lo_half, zero, a)], axis=0)
        return g0.astype(BF16), g1.astype(BF16)

    k_prev = kprev_ref[...]
    v_prev = vprev_ref[...]
    kbs, vbs, biases = [], [], []
    for j in range(n_blocks):
        rows = slice(j * ATTN_BLOCK, (j + 1) * ATTN_BLOCK)
        k_cur, v_cur = k_all[rows], v_all[rows]
        kbs.append(block_diag(jnp.concatenate([k_prev, k_cur], axis=0)))
        vbs.append(block_diag(jnp.concatenate([v_prev, v_cur], axis=0)))
        if j == 0:
            first_key = jnp.where(t == 0, ATTN_BLOCK, 0)
            biases.append(jnp.where(band & (col >= first_key), 0.0, -jnp.inf))
        else:
            biases.append(jnp.where(band, 0.0, -jnp.inf))
        k_prev, v_prev = k_cur, v_cur
    kprev_ref[...] = k_prev
    vprev_ref[...] = v_prev

    def scores(j, p):
        rows = slice(j * ATTN_BLOCK, (j + 1) * ATTN_BLOCK)
        qp = (rope(proj_ref[rows, p * LANES:(p + 1) * LANES], rows) * scale).astype(BF16)
        return _dot_nt(qp, kbs[j][(2 * p) // GROUP]) + biases[j]

    items = [(j, p) for j in range(n_blocks) for p in range(N_HEADS // 2)]
    s_next = scores(*items[0])
    for i, (j, p) in enumerate(items):
        s = s_next
        if i + 1 < len(items):
            s_next = scores(*items[i + 1])
        halves = []
        invs = []
        for e in range(2):
            se = s[:, e * 2 * ATTN_BLOCK:(e + 1) * 2 * ATTN_BLOCK]
            sink = sinks_ref[2 * p + e]
            m = jnp.maximum(jnp.max(se, axis=-1, keepdims=True), sink)
            pe = jnp.exp(se - m)
            denom = jnp.sum(pe, axis=-1, keepdims=True) + jnp.exp(sink - m)
            halves.append(pe.astype(BF16))
            invs.append(1.0 / denom)
        o = _dot(jnp.concatenate(halves, axis=1), vbs[j][(2 * p) // GROUP])
        att_ref[j * ATTN_BLOCK:(j + 1) * ATTN_BLOCK, p * LANES:(p + 1) * LANES] = (
            o * jnp.where(lo_half, invs[0], invs[1]))

    z = proj_ref[:, ATTN_WIDTH + 2 * KV_WIDTH:]
    gated = (att_ref[...] * (z * _sigmoid(z))).astype(BF16)
    y = _dot(gated, wout_ref[...]) + bout_ref[...]
    o_ref[...] = x + _rmsnorm(y, postw_ref[...])


def _attn_layer(x, positions, pre_w, post_w, w_in, b_in, sinks, w_out, b_out):
    B, T, D = x.shape
    tq = min(ATTN_TILE, T)
    assert T % tq == 0 and tq % ATTN_BLOCK == 0 and D == D_MODEL
    inv_freq = ROPE_THETA ** (-(jnp.arange(ROPE_HALF, dtype=F32) * 2.0 / ROPE_DIM))
    invf = jnp.tile(inv_freq, LANES // ROPE_HALF).reshape(1, LANES)
    const = lambda b, t: (0, 0)
    return pl.pallas_call(
        _attn_layer_kernel,
        out_shape=jax.ShapeDtypeStruct((B, T, D), x.dtype),
        grid=(B, T // tq),
        in_specs=[
            pl.BlockSpec(memory_space=pltpu.SMEM),
            pl.BlockSpec((None, tq, D), lambda b, t: (b, t, 0)),
            pl.BlockSpec((None, tq, 1), lambda b, t: (b, t, 0)),
            pl.BlockSpec((1, LANES), const),
            pl.BlockSpec((1, D), const),
            pl.BlockSpec((1, D), const),
            pl.BlockSpec((D, ATTN_IN), const),
            pl.BlockSpec((1, ATTN_IN), const),
            pl.BlockSpec((ATTN_WIDTH, D), const),
            pl.BlockSpec((1, D), const),
        ],
        out_specs=pl.BlockSpec((None, tq, D), lambda b, t: (b, t, 0)),
        scratch_shapes=[
            pltpu.VMEM((ATTN_BLOCK, KV_WIDTH), F32),
            pltpu.VMEM((ATTN_BLOCK, KV_WIDTH), F32),
            pltpu.VMEM((tq, ATTN_IN), F32),
            pltpu.VMEM((tq, ATTN_WIDTH), F32),
        ],
        compiler_params=pltpu.CompilerParams(
            dimension_semantics=("arbitrary", "arbitrary"),
            vmem_limit_bytes=VMEM_LIMIT_BYTES),
        name="swa_layer",
    )(sinks.astype(F32), x, positions.reshape(B, T, 1), invf, pre_w.reshape(1, D), post_w.reshape(1, D),
      w_in.astype(BF16), b_in.reshape(1, ATTN_IN), w_out.astype(BF16), b_out.reshape(1, D))


def _boundary_rows(b, m):
    c, n = b.shape
    if 2 * m >= SUBLANES:
        b3 = b.reshape(c // (2 * m), 2 * m, n)
        return jnp.broadcast_to(b3[:, m - 1:m, :], b3.shape).reshape(c, n)
    b3 = b.reshape(c // SUBLANES, SUBLANES, n)
    sub = lax.broadcasted_iota(jnp.int32, b3.shape, 1)
    r = jnp.broadcast_to(b3[:, SUBLANES - m - 1:SUBLANES - m, :], b3.shape)
    for start in range(SUBLANES - 4 * m, -1, -2 * m):
        r = jnp.where(sub < start + 2 * m, jnp.broadcast_to(b3[:, start + m - 1:start + m, :], b3.shape), r)
    return r.reshape(c, n)


def _rec_layer_kernel(layer, x_ref, lbl_ref, lv_ref, ltri_ref, prew_ref, postw_ref, win_ref, gnw_ref,
                      wout_ref, o_ref, state_ref, proj_ref, q_ref, k_ref, v_ref, ghi_ref, glo_ref, b_ref,
                      gated_ref):
    c = pl.program_id(1)
    chunk = x_ref.shape[0]
    seg = lv_ref.shape[0]
    seg_levels = seg.bit_length() - 1

    @pl.when(c == 0)
    def _():
        state_ref[...] = jnp.zeros_like(state_ref)

    x = x_ref[...]
    h = _rmsnorm(x, prew_ref[...]).astype(BF16)
    proj_ref[...] = _dot(h, win_ref[...])

    def gates(hd, carry):
        off = pl.multiple_of(hd * LANES, LANES)
        q_raw = proj_ref[:, pl.ds(off, LANES)]
        f_raw = proj_ref[:, pl.ds(FORGET_DIM + off, LANES)]

        logits = lbl_ref[:, pl.ds(off, LANES)]
        ex = jnp.exp(logits - jnp.max(logits, axis=0, keepdims=True))
        lb = jnp.sum(ex[1:layer + 1], axis=0, keepdims=True) / jnp.sum(ex, axis=0, keepdims=True)

        e = jnp.exp(-jnp.abs(f_raw))
        sp = 1.0 / (1.0 + e)
        pos = f_raw >= 0
        sig = jnp.where(pos, sp, e * sp)
        nsig = jnp.where(pos, e * sp, sp)
        log2_f = jnp.log2(lb + (1.0 - lb) * sig)
        q_ref[:, pl.ds(off, LANES)] = (q_raw * _sigmoid(q_raw)).astype(BF16)
        k_ref[:, pl.ds(off, LANES)] = ((1.0 - lb) * nsig).astype(BF16)
        v_ref[:, pl.ds(off, LANES)] = proj_ref[:, pl.ds(2 * FORGET_DIM + off, LANES)].astype(BF16)
        g_hi = log2_f.astype(BF16)
        ghi_ref[:, pl.ds(off, LANES)] = g_hi
        glo_ref[:, pl.ds(off, LANES)] = (log2_f - g_hi.astype(F32)).astype(BF16)
        return carry

    lax.fori_loop(0, REC_HEADS, gates, 0, unroll=True)

    ltri = ltri_ref[...]
    b_ref[...] = _dot(ltri, ghi_ref[...]) + _dot(ltri, glo_ref[...])

    def head(hd, carry):
        off = pl.multiple_of(hd * LANES, LANES)
        b = b_ref[:, pl.ds(off, LANES)]
        q = q_ref[:, pl.ds(off, LANES)]
        k = k_ref[:, pl.ds(off, LANES)]
        v = v_ref[:, pl.ds(off, LANES)]
        lv = lv_ref[...]

        outs = []
        for s0 in range(0, chunk, seg):
            rows = slice(s0, s0 + seg)
            qs, ks, bs = q[rows], k[rows], b[rows]
            a = jnp.where(lv == -1, _dot_nt(qs, ks), 0.0)
            for j in range(seg_levels):
                en = jnp.exp2(-jnp.abs(bs - _boundary_rows(bs, 1 << j))).astype(BF16)
                a = jnp.where(lv == j, _dot_nt(qs * en, ks * en), a)
            outs.append(_dot(a.astype(BF16), v[rows]))
        m = seg
        while m < chunk:
            for base in range(0, chunk, 2 * m):
                lo, hi = slice(base, base + m), slice(base + m, base + 2 * m)
                r = b[base + m - 1:base + m]
                qh = q[hi] * jnp.exp2(b[hi] - r).astype(BF16)
                kh = k[lo] * jnp.exp2(r - b[lo]).astype(BF16)
                upd = _dot(_dot_nt(qh, kh).astype(BF16), v[lo])
                for i in range(m // seg):
                    outs[(base + m) // seg + i] += upd[i * seg:(i + 1) * seg]
            m *= 2
        o = jnp.concatenate(outs, axis=0)

        b_last = b[chunk - 1:chunk, :]
        st = state_ref[hd]
        o = o + _dot_nt(q * jnp.exp2(b).astype(BF16), st.astype(BF16))
        kd = k * jnp.exp2(b_last - b).astype(BF16)
        state_ref[hd] = st * jnp.exp2(b_last) + _dot_tn(v, kd)

        o = o * lax.rsqrt(jnp.mean(o * o, axis=-1, keepdims=True) + NORM_EPS) * gnw_ref[...]
        z = proj_ref[:, pl.ds(2 * FORGET_DIM + REC_WIDTH + off, LANES)]
        gated_ref[:, pl.ds(off, LANES)] = (o * (z * _sigmoid(z))).astype(BF16)
        return carry

    lax.fori_loop(0, REC_HEADS, head, 0, unroll=True)

    y = _dot(gated_ref[...], wout_ref[...])
    o_ref[...] = x + _rmsnorm(y, postw_ref[...])


def _level_index(c):
    t = np.arange(c)[:, None]
    s = np.arange(c)[None, :]
    lv = np.floor(np.log2(np.maximum(t ^ s, 1))).astype(np.int32)
    lv = np.where(s < t, lv, np.where(s == t, -1, 99))
    return jnp.asarray(lv, dtype=jnp.int32)


def _rec_layer(x, layer, lb_logits, pre_w, post_w, w_in, gnorm_w, w_out):
    B, T, D = x.shape
    chunk = min(REC_CHUNK, T)
    seg = min(REC_SEG, chunk)
    assert T % chunk == 0 and chunk & (chunk - 1) == 0 and seg >= 2 * SUBLANES and D == D_MODEL
    depth = lb_logits.shape[0]
    ltri = jnp.asarray(np.tril(np.ones((chunk, chunk), np.float32)), dtype=BF16)
    const = lambda b, t: (0, 0)
    return pl.pallas_call(
        functools.partial(_rec_layer_kernel, layer),
        out_shape=jax.ShapeDtypeStruct((B, T, D), x.dtype),
        grid=(B, T // chunk),
        in_specs=[
            pl.BlockSpec((None, chunk, D), lambda b, t: (b, t, 0)),
            pl.BlockSpec((depth, FORGET_DIM), const),
            pl.BlockSpec((seg, seg), const),
            pl.BlockSpec((chunk, chunk), const),
            pl.BlockSpec((1, D), const),
            pl.BlockSpec((1, D), const),
            pl.BlockSpec((D, REC_IN), const),
            pl.BlockSpec((1, REC_VALUE_DIM), const),
            pl.BlockSpec((REC_WIDTH, D), const),
        ],
        out_specs=pl.BlockSpec((None, chunk, D), lambda b, t: (b, t, 0)),
        scratch_shapes=[
            pltpu.VMEM((REC_HEADS, REC_VALUE_DIM, REC_KEY_DIM), F32),
            pltpu.VMEM((chunk, REC_IN), F32),
            pltpu.VMEM((chunk, FORGET_DIM), BF16),
            pltpu.VMEM((chunk, FORGET_DIM), BF16),
            pltpu.VMEM((chunk, REC_WIDTH), BF16),
            pltpu.VMEM((chunk, FORGET_DIM), BF16),
            pltpu.VMEM((chunk, FORGET_DIM), BF16),
            pltpu.VMEM((chunk, FORGET_DIM), F32),
            pltpu.VMEM((chunk, REC_WIDTH), BF16),
        ],
        compiler_params=pltpu.CompilerParams(
            dimension_semantics=("arbitrary", "arbitrary"),
            vmem_limit_bytes=VMEM_LIMIT_BYTES),
        name="hgrn2_layer",
    )(x, lb_logits.astype(F32), _level_index(seg), ltri, pre_w.reshape(1, D), post_w.reshape(1, D),
      w_in.astype(BF16), gnorm_w.reshape(1, REC_VALUE_DIM), w_out.astype(BF16))


def kernel(x, positions, pre_norm_w, post_norm_w, attn_w_in, attn_b_in, attn_sinks, attn_w_out, attn_b_out,
           rec_w_in, rec_lb_logits, rec_gnorm_w, rec_w_out):
    depth = pre_norm_w.shape[0]
    for layer in range(depth):
        j = layer // N_MIXERS
        if layer % N_MIXERS == 0:
            x = _attn_layer(x, positions, pre_norm_w[layer], post_norm_w[layer], attn_w_in[j], attn_b_in[j],
                            attn_sinks[j], attn_w_out[j], attn_b_out[j])
        else:
            x = _rec_layer(x, layer, rec_lb_logits, pre_norm_w[layer], post_norm_w[layer], rec_w_in[j],
                           rec_gnorm_w[j], rec_w_out[j])
    return x
```

```python
import functools

import numpy as np
import jax
import jax.numpy as jnp
from jax import lax
from jax.experimental import pallas as pl
from jax.experimental.pallas import tpu as pltpu

D_MODEL = 1024
N_MIXERS = 2

HEAD_DIM = 64
N_HEADS = D_MODEL // HEAD_DIM
N_KV_HEADS = 2
GROUP = N_HEADS // N_KV_HEADS
ATTN_WIDTH = N_HEADS * HEAD_DIM
KV_WIDTH = N_KV_HEADS * HEAD_DIM
ATTN_IN = 2 * ATTN_WIDTH + 2 * KV_WIDTH
ATTN_BLOCK = 128
ROPE_THETA = 500000.0
ROPE_DIM = HEAD_DIM // 4
ROPE_HALF = ROPE_DIM // 2

REC_HEADS = 8
REC_KEY_DIM = 128
REC_VALUE_DIM = D_MODEL // REC_HEADS
FORGET_DIM = REC_HEADS * REC_KEY_DIM
REC_WIDTH = REC_HEADS * REC_VALUE_DIM
REC_IN = 2 * FORGET_DIM + 2 * REC_WIDTH

NORM_EPS = 1e-6
LOG2_E = 1.4426950408889634

LANES = 128
SUBLANES = 8
ATTN_TILE = 256
REC_CHUNK = 256
REC_SEG = 64
REC_BASE = 32
SAFE_LOG2_SPREAD = 96.0
VMEM_LIMIT_BYTES = 48 * 1024 * 1024

F32 = jnp.float32
BF16 = jnp.bfloat16


def _rmsnorm(x, w):
    return x * lax.rsqrt(jnp.mean(x * x, axis=-1, keepdims=True) + NORM_EPS) * w


def _sigmoid(x):
    return 1.0 / (1.0 + jnp.exp(-x))


def _dot(a, b):
    return jnp.dot(a, b, preferred_element_type=F32)


def _dot_nt(a, b):
    return lax.dot_general(a, b, (((1,), (1,)), ((), ())), preferred_element_type=F32)


def _dot_tn(a, b):
    return lax.dot_general(a, b, (((0,), (0,)), ((), ())), preferred_element_type=F32)


def _split_dot_tn(x, p):
    acc = None
    for _ in range(3):
        part = x.astype(BF16)
        x = x - part.astype(F32)
        term = _dot_tn(part, p)
        acc = term if acc is None else acc + term
    return acc


def _attn_layer_kernel(sinks_ref, x_ref, pos_ref, invf_ref, spread_ref, prew_ref, postw_ref, win_ref, bin_ref,
                       wout_ref, bout_ref, o_ref, kprev_ref, vprev_ref, proj_ref, att_ref):
    t = pl.program_id(1)
    tq = x_ref.shape[0]
    n_blocks = tq // ATTN_BLOCK

    @pl.when(t == 0)
    def _():
        kprev_ref[...] = jnp.zeros_like(kprev_ref)
        vprev_ref[...] = jnp.zeros_like(vprev_ref)

    x = x_ref[...]
    h = _rmsnorm(x, prew_ref[...]).astype(BF16)

    def project(c0, c1):
        proj_ref[:, c0:c1] = _dot(h, win_ref[:, c0:c1]) + bin_ref[:, c0:c1]

    q_cols = GROUP * HEAD_DIM
    z0 = ATTN_WIDTH + 2 * KV_WIDTH
    project(ATTN_WIDTH, z0)
    project(0, q_cols)
    later_chunks = [(g * q_cols, (g + 1) * q_cols) for g in range(1, N_KV_HEADS)]
    later_chunks += [(z0, z0 + ATTN_WIDTH // 2), (z0 + ATTN_WIDTH // 2, z0 + ATTN_WIDTH)]

    ang_t = invf_ref[...] * pos_ref[...].astype(F32)
    trig_t = jnp.concatenate([jnp.cos(ang_t), jnp.sin(ang_t)], axis=0)
    tabs = _split_dot_tn(trig_t, spread_ref[...])
    lane = lax.broadcasted_iota(jnp.int32, (1, LANES), 1)
    d = lane % HEAD_DIM
    c_tab = tabs[:, :LANES] + jnp.where(d < ROPE_DIM, 0.0, 1.0)
    s_lo = tabs[:, LANES:2 * LANES]
    s_hi = tabs[:, 2 * LANES:]

    def rope(xc, rows=slice(None)):
        return (xc * c_tab[rows] + pltpu.roll(xc, LANES - ROPE_HALF, 1) * s_lo[rows]
                + pltpu.roll(xc, ROPE_HALF, 1) * s_hi[rows])

    k_all = rope(proj_ref[:, ATTN_WIDTH:ATTN_WIDTH + KV_WIDTH])
    v_all = proj_ref[:, ATTN_WIDTH + KV_WIDTH:ATTN_WIDTH + 2 * KV_WIDTH]

    lo_half = lane < HEAD_DIM
    from_prev = (lax.broadcasted_iota(jnp.int32, (ATTN_BLOCK, ATTN_BLOCK), 1)
                 > lax.broadcasted_iota(jnp.int32, (ATTN_BLOCK, ATTN_BLOCK), 0))
    no_prev = jnp.where(t == 0, -jnp.inf, 0.0)
    scale = HEAD_DIM ** -0.5 * LOG2_E

    def block_diag(a):
        a_sw = pltpu.roll(a, HEAD_DIM, 1)
        zero = jnp.zeros_like(a)
        g0 = jnp.concatenate([jnp.where(lo_half, a, zero), jnp.where(lo_half, zero, a_sw)], axis=0)
        g1 = jnp.concatenate([jnp.where(lo_half, a_sw, zero), jnp.where(lo_half, zero, a)], axis=0)
        return g0.astype(BF16), g1.astype(BF16)

    k_prev = kprev_ref[...]
    v_prev = vprev_ref[...]
    kbs, vbs = [], []
    for j in range(n_blocks):
        rows = slice(j * ATTN_BLOCK, (j + 1) * ATTN_BLOCK)
        k_cur, v_cur = k_all[rows], v_all[rows]
        kbs.append(block_diag(jnp.concatenate([k_prev, k_cur], axis=0)))
        vbs.append(block_diag(jnp.concatenate([v_prev, v_cur], axis=0)))
        k_prev, v_prev = k_cur, v_cur
    kprev_ref[...] = k_prev
    vprev_ref[...] = v_prev

    pairs_per_group = GROUP // 2

    def scores(j, g):
        rows = slice(j * ATTN_BLOCK, (j + 1) * ATTN_BLOCK)
        qs = [(rope(proj_ref[rows, p * LANES:(p + 1) * LANES], rows) * scale).astype(BF16)
              for p in range(g * pairs_per_group, (g + 1) * pairs_per_group)]
        return _dot_nt(jnp.concatenate(qs, axis=0), kbs[j][g])

    items = [(j, g) for g in range(N_KV_HEADS) for j in range(n_blocks)]
    s_next = scores(*items[0])
    for i, (j, g) in enumerate(items):
        s_group = s_next
        if later_chunks:
            project(*later_chunks.pop(0))
        if i + 1 < len(items):
            s_next = scores(*items[i + 1])
        probs, invs = [], []
        for pi in range(pairs_per_group):
            p = g * pairs_per_group + pi
            s = s_group[pi * ATTN_BLOCK:(pi + 1) * ATTN_BLOCK]
            halves, inv = [], []
            for e in range(2):
                s_prev = s[:, 2 * e * ATTN_BLOCK:(2 * e + 1) * ATTN_BLOCK]
                s_cur = s[:, (2 * e + 1) * ATTN_BLOCK:(2 * e + 2) * ATTN_BLOCK]
                if j == 0:
                    s_prev = s_prev + no_prev
                sw = jnp.where(from_prev, s_prev, s_cur)
                sink = sinks_ref[2 * p + e] * LOG2_E
                m = jnp.maximum(jnp.max(sw, axis=-1, keepdims=True), sink)
                pw = jnp.exp2(sw - m)
                denom = jnp.sum(pw, axis=-1, keepdims=True) + jnp.exp2(sink - m)
                pw = pw.astype(BF16)
                zero = jnp.zeros_like(pw)
                halves += [jnp.where(from_prev, pw, zero), jnp.where(from_prev, zero, pw)]
                inv.append(1.0 / denom)
            probs.append(jnp.concatenate(halves, axis=1))
            invs.append(jnp.where(lo_half, inv[0], inv[1]))
        o = _dot(jnp.concatenate(probs, axis=0), vbs[j][g])
        for pi in range(pairs_per_group):
            p = g * pairs_per_group + pi
            att_ref[j * ATTN_BLOCK:(j + 1) * ATTN_BLOCK, p * LANES:(p + 1) * LANES] = (
                o[pi * ATTN_BLOCK:(pi + 1) * ATTN_BLOCK] * invs[pi])
    for chunk in later_chunks:
        project(*chunk)

    z = proj_ref[:, z0:]
    gated = (att_ref[...] * (z * _sigmoid(z))).astype(BF16)
    y = _dot(gated, wout_ref[...]) + bout_ref[...]
    o_ref[...] = x + _rmsnorm(y, postw_ref[...])


def _rope_spread_matrix():
    p = np.zeros((2 * ROPE_HALF, 3 * LANES), np.float32)
    for lane in range(LANES):
        d = lane % HEAD_DIM
        if d < ROPE_DIM:
            p[d % ROPE_HALF, lane] = 1.0
        if d < ROPE_HALF:
            p[ROPE_HALF + d, LANES + lane] = -1.0
        elif d < ROPE_DIM:
            p[ROPE_HALF + d - ROPE_HALF, 2 * LANES + lane] = 1.0
    return jnp.asarray(p, dtype=BF16)


def _attn_layer(x, positions, pre_w, post_w, w_in, b_in, sinks, w_out, b_out):
    B, T, D = x.shape
    tq = min(ATTN_TILE, T)
    assert T % tq == 0 and tq % ATTN_BLOCK == 0 and D == D_MODEL
    inv_freq = ROPE_THETA ** (-(jnp.arange(ROPE_HALF, dtype=F32) * 2.0 / ROPE_DIM))
    const = lambda b, t: (0, 0)
    return pl.pallas_call(
        _attn_layer_kernel,
        out_shape=jax.ShapeDtypeStruct((B, T, D), x.dtype),
        grid=(B, T // tq),
        in_specs=[
            pl.BlockSpec(memory_space=pltpu.SMEM),
            pl.BlockSpec((None, tq, D), lambda b, t: (b, t, 0)),
            pl.BlockSpec((None, 1, tq), lambda b, t: (b, 0, t)),
            pl.BlockSpec((ROPE_HALF, 1), const),
            pl.BlockSpec((2 * ROPE_HALF, 3 * LANES), const),
            pl.BlockSpec((1, D), const),
            pl.BlockSpec((1, D), const),
            pl.BlockSpec((D, ATTN_IN), const),
            pl.BlockSpec((1, ATTN_IN), const),
            pl.BlockSpec((ATTN_WIDTH, D), const),
            pl.BlockSpec((1, D), const),
        ],
        out_specs=pl.BlockSpec((None, tq, D), lambda b, t: (b, t, 0)),
        scratch_shapes=[
            pltpu.VMEM((ATTN_BLOCK, KV_WIDTH), F32),
            pltpu.VMEM((ATTN_BLOCK, KV_WIDTH), F32),
            pltpu.VMEM((tq, ATTN_IN), F32),
            pltpu.VMEM((tq, ATTN_WIDTH), F32),
        ],
        compiler_params=pltpu.CompilerParams(
            dimension_semantics=("arbitrary", "arbitrary"),
            vmem_limit_bytes=VMEM_LIMIT_BYTES),
        name="swa_layer",
    )(sinks.astype(F32), x, positions.reshape(B, 1, T), inv_freq.reshape(ROPE_HALF, 1), _rope_spread_matrix(),
      pre_w.reshape(1, D), post_w.reshape(1, D),
      w_in.astype(BF16), b_in.reshape(1, ATTN_IN), w_out.astype(BF16), b_out.reshape(1, D))


def _boundary_rows(b, m):
    c, n = b.shape
    if 2 * m >= SUBLANES:
        b3 = b.reshape(c // (2 * m), 2 * m, n)
        return jnp.broadcast_to(b3[:, m - 1:m, :], b3.shape).reshape(c, n)
    b3 = b.reshape(c // SUBLANES, SUBLANES, n)
    sub = lax.broadcasted_iota(jnp.int32, b3.shape, 1)
    r = jnp.broadcast_to(b3[:, SUBLANES - m - 1:SUBLANES - m, :], b3.shape)
    for start in range(SUBLANES - 4 * m, -1, -2 * m):
        r = jnp.where(sub < start + 2 * m, jnp.broadcast_to(b3[:, start + m - 1:start + m, :], b3.shape), r)
    return r.reshape(c, n)


def _rec_layer_kernel(layer, x_ref, lbl_ref, lv_ref, ltri_ref, prew_ref, postw_ref, win_ref, gnw_ref,
                      wout_ref, o_ref, state_ref, proj_ref, q_ref, k_ref, ghi_ref, glo_ref, b_ref,
                      gated_ref):
    c = pl.program_id(1)
    chunk = x_ref.shape[0]
    seg = lv_ref.shape[0]
    seg_levels = seg.bit_length() - 1
    base_levels = REC_BASE.bit_length() - 1

    @pl.when(c == 0)
    def _():
        state_ref[...] = jnp.zeros_like(state_ref)

    x = x_ref[...]
    h = _rmsnorm(x, prew_ref[...]).astype(BF16)

    def project(c0, c1):
        proj_ref[:, c0:c1] = _dot(h, win_ref[:, c0:c1])

    project(FORGET_DIM, 2 * FORGET_DIM)
    project(0, FORGET_DIM)

    def forget_gates(hd):
        off = hd * LANES
        f_raw = proj_ref[:, FORGET_DIM + off:FORGET_DIM + off + LANES]

        logits = lbl_ref[:, off:off + LANES]
        ex = jnp.exp(logits - jnp.max(logits, axis=0, keepdims=True))
        lb = jnp.sum(ex[1:layer + 1], axis=0, keepdims=True) / jnp.sum(ex, axis=0, keepdims=True)

        e = jnp.exp(-jnp.abs(f_raw))
        sp = 1.0 / (1.0 + e)
        pos = f_raw >= 0
        sig = jnp.where(pos, sp, e * sp)
        nsig = jnp.where(pos, e * sp, sp)
        log2_f = jnp.log2(lb + (1.0 - lb) * sig)
        k_ref[:, off:off + LANES] = ((1.0 - lb) * nsig).astype(BF16)
        g_hi = log2_f.astype(BF16)
        ghi_ref[:, off:off + LANES] = g_hi
        glo_ref[:, off:off + LANES] = (log2_f - g_hi.astype(F32)).astype(BF16)

    for hd in range(REC_HEADS):
        forget_gates(hd)
    project(2 * FORGET_DIM, 2 * FORGET_DIM + REC_WIDTH)
    project(2 * FORGET_DIM + REC_WIDTH, REC_IN)

    ltri = ltri_ref[...]
    b_all = _dot(ltri, ghi_ref[...]) + _dot(ltri, glo_ref[...])
    for hd in range(REC_HEADS):
        b_ref[hd] = b_all[:, hd * LANES:(hd + 1) * LANES]
        q_raw = proj_ref[:, hd * LANES:(hd + 1) * LANES]
        q_ref[:, hd * LANES:(hd + 1) * LANES] = (q_raw * _sigmoid(q_raw)).astype(BF16)

    def head(hd, bounded):
        off = hd * LANES
        b = b_ref[hd]
        q = q_ref[:, off:off + LANES]
        k = k_ref[:, off:off + LANES]
        v = proj_ref[:, 2 * FORGET_DIM + off:2 * FORGET_DIM + off + LANES].astype(BF16)
        lv = lv_ref[...]

        outs = []
        for s0 in range(0, chunk, seg):
            rows = slice(s0, s0 + seg)
            qs, ks, bs = q[rows], k[rows], b[rows]
            if bounded:
                b3 = bs.reshape(seg // REC_BASE, REC_BASE, LANES)
                r = jnp.broadcast_to(b3[:, 0:1, :], b3.shape).reshape(seg, LANES)
                a = jnp.where(lv < base_levels,
                              _dot_nt(qs * jnp.exp2(bs - r).astype(BF16), ks * jnp.exp2(r - bs).astype(BF16)), 0.0)
                first_level = base_levels
            else:
                a = jnp.where(lv == -1, _dot_nt(qs, ks), 0.0)
                first_level = 0
            for j in range(first_level, seg_levels):
                en = jnp.exp2(-jnp.abs(bs - _boundary_rows(bs, 1 << j))).astype(BF16)
                a = jnp.where(lv == j, _dot_nt(qs * en, ks * en), a)
            outs.append(_dot(a.astype(BF16), v[rows]))
        m = seg
        while m < chunk:
            for base in range(0, chunk, 2 * m):
                lo, hi = slice(base, base + m), slice(base + m, base + 2 * m)
                r = b[base + m - 1:base + m]
                qh = q[hi] * jnp.exp2(b[hi] - r).astype(BF16)
                kh = k[lo] * jnp.exp2(r - b[lo]).astype(BF16)
                upd = _dot(_dot_nt(qh, kh).astype(BF16), v[lo])
                for i in range(m // seg):
                    outs[(base + m) // seg + i] += upd[i * seg:(i + 1) * seg]
            m *= 2
        o = jnp.concatenate(outs, axis=0)

        b_last = b[chunk - 1:chunk, :]
        st = state_ref[hd]
        o = o + _dot_nt(q * jnp.exp2(b).astype(BF16), st.astype(BF16))
        kd = k * jnp.exp2(b_last - b).astype(BF16)
        state_ref[hd] = st * jnp.exp2(b_last) + _dot_tn(v, kd)

        o = o * lax.rsqrt(jnp.mean(o * o, axis=-1, keepdims=True) + NORM_EPS) * gnw_ref[...]
        z_off = 2 * FORGET_DIM + REC_WIDTH + off
        z = proj_ref[:, z_off:z_off + LANES]
        gated_ref[:, off:off + LANES] = (o * (z * _sigmoid(z))).astype(BF16)

    def all_heads(bounded):
        for hd in range(REC_HEADS):
            head(hd, bounded)

    n_base = chunk // REC_BASE
    spread = jnp.max(b_ref[:, pl.ds(0, n_base, stride=REC_BASE), :]
                     - b_ref[:, pl.ds(REC_BASE - 1, n_base, stride=REC_BASE), :])
    lax.cond(spread <= SAFE_LOG2_SPREAD, lambda: all_heads(True), lambda: all_heads(False))

    y = _dot(gated_ref[...], wout_ref[...])
    o_ref[...] = x + _rmsnorm(y, postw_ref[...])


def _level_index(c):
    t = np.arange(c)[:, None]
    s = np.arange(c)[None, :]
    lv = np.floor(np.log2(np.maximum(t ^ s, 1))).astype(np.int32)
    lv = np.where(s < t, lv, np.where(s == t, -1, 99))
    return jnp.asarray(lv, dtype=jnp.int32)


def _rec_layer(x, layer, lb_logits, pre_w, post_w, w_in, gnorm_w, w_out):
    B, T, D = x.shape
    chunk = min(REC_CHUNK, T)
    seg = min(REC_SEG, chunk)
    assert T % chunk == 0 and chunk & (chunk - 1) == 0 and seg >= 2 * SUBLANES and D == D_MODEL
    depth = lb_logits.shape[0]
    ltri = jnp.asarray(np.tril(np.ones((chunk, chunk), np.float32)), dtype=BF16)
    const = lambda b, t: (0, 0)
    return pl.pallas_call(
        functools.partial(_rec_layer_kernel, layer),
        out_shape=jax.ShapeDtypeStruct((B, T, D), x.dtype),
        grid=(B, T // chunk),
        in_specs=[
            pl.BlockSpec((None, chunk, D), lambda b, t: (b, t, 0)),
            pl.BlockSpec((depth, FORGET_DIM), const),
            pl.BlockSpec((seg, seg), const),
            pl.BlockSpec((chunk, chunk), const),
            pl.BlockSpec((1, D), const),
            pl.BlockSpec((1, D), const),
            pl.BlockSpec((D, REC_IN), const),
            pl.BlockSpec((1, REC_VALUE_DIM), const),
            pl.BlockSpec((REC_WIDTH, D), const),
        ],
        out_specs=pl.BlockSpec((None, chunk, D), lambda b, t: (b, t, 0)),
        scratch_shapes=[
            pltpu.VMEM((REC_HEADS, REC_VALUE_DIM, REC_KEY_DIM), F32),
            pltpu.VMEM((chunk, REC_IN), F32),
            pltpu.VMEM((chunk, FORGET_DIM), BF16),
            pltpu.VMEM((chunk, FORGET_DIM), BF16),
            pltpu.VMEM((chunk, FORGET_DIM), BF16),
            pltpu.VMEM((chunk, FORGET_DIM), BF16),
            pltpu.VMEM((REC_HEADS, chunk, REC_KEY_DIM), F32),
            pltpu.VMEM((chunk, REC_WIDTH), BF16),
        ],
        compiler_params=pltpu.CompilerParams(
            dimension_semantics=("arbitrary", "arbitrary"),
            vmem_limit_bytes=VMEM_LIMIT_BYTES),
        name="hgrn2_layer",
    )(x, lb_logits.astype(F32), _level_index(seg), ltri, pre_w.reshape(1, D), post_w.reshape(1, D),
      w_in.astype(BF16), gnorm_w.reshape(1, REC_VALUE_DIM), w_out.astype(BF16))


def kernel(x, positions, pre_norm_w, post_norm_w, attn_w_in, attn_b_in, attn_sinks, attn_w_out, attn_b_out,
           rec_w_in, rec_lb_logits, rec_gnorm_w, rec_w_out):
    depth = pre_norm_w.shape[0]
    for layer in range(depth):
        j = layer // N_MIXERS
        if layer % N_MIXERS == 0:
            x = _attn_layer(x, positions, pre_norm_w[layer], post_norm_w[layer], attn_w_in[j], attn_b_in[j],
                            attn_sinks[j], attn_w_out[j], attn_b_out[j])
        else:
            x = _rec_layer(x, layer, rec_lb_logits, pre_norm_w[layer], post_norm_w[layer], rec_w_in[j],
                           rec_gnorm_w[j], rec_w_out[j])
    return x
```

```python
import functools

import numpy as np
import jax
import jax.numpy as jnp
from jax import lax
from jax.experimental import pallas as pl
from jax.experimental.pallas import tpu as pltpu

D_MODEL = 1024
N_MIXERS = 2

HEAD_DIM = 64
N_HEADS = D_MODEL // HEAD_DIM
N_KV_HEADS = 2
GROUP = N_HEADS // N_KV_HEADS
ATTN_WIDTH = N_HEADS * HEAD_DIM
KV_WIDTH = N_KV_HEADS * HEAD_DIM
ATTN_IN = 2 * ATTN_WIDTH + 2 * KV_WIDTH
ATTN_BLOCK = 128
ROPE_THETA = 500000.0
ROPE_DIM = HEAD_DIM // 4
ROPE_HALF = ROPE_DIM // 2

REC_HEADS = 8
REC_KEY_DIM = 128
REC_VALUE_DIM = D_MODEL // REC_HEADS
FORGET_DIM = REC_HEADS * REC_KEY_DIM
REC_WIDTH = REC_HEADS * REC_VALUE_DIM
REC_IN = 2 * FORGET_DIM + 2 * REC_WIDTH

NORM_EPS = 1e-6
LOG2_E = 1.4426950408889634

LANES = 128
SUBLANES = 8
ATTN_TILE = 256
REC_CHUNK = 256
STEP_TILES = 4
REC_SEG = 64
REC_BASE = 32
SAFE_LOG2_SPREAD = 96.0
VMEM_LIMIT_BYTES = 48 * 1024 * 1024

F32 = jnp.float32
BF16 = jnp.bfloat16


def _step_rows(total, tile):
    n = min(STEP_TILES, total // tile)
    while (total // tile) % n:
        n -= 1
    return n * tile


def _rmsnorm(x, w):
    return x * lax.rsqrt(jnp.mean(x * x, axis=-1, keepdims=True) + NORM_EPS) * w


def _sigmoid(x):
    return 1.0 / (1.0 + jnp.exp(-x))


def _dot(a, b):
    return jnp.dot(a, b, preferred_element_type=F32)


def _dot_nt(a, b):
    return lax.dot_general(a, b, (((1,), (1,)), ((), ())), preferred_element_type=F32)


def _dot_tn(a, b):
    return lax.dot_general(a, b, (((0,), (0,)), ((), ())), preferred_element_type=F32)


def _split_dot_tn(x, p):
    acc = None
    for _ in range(3):
        part = x.astype(BF16)
        x = x - part.astype(F32)
        term = _dot_tn(part, p)
        acc = term if acc is None else acc + term
    return acc


def _attn_layer_kernel(tq, sinks_ref, x_ref, pos_ref, *rest):
    *consts, o_ref, kprev_ref, vprev_ref, proj_ref, att_ref = rest
    n_sub = x_ref.shape[0] // tq

    def sub_tile(i, carry):
        r0 = pl.multiple_of(i * tq, tq)
        _attn_tile(pl.program_id(1) * n_sub + i, sinks_ref, x_ref.at[pl.ds(r0, tq)],
                   pos_ref.at[:, pl.ds(r0, tq)], *consts, o_ref.at[pl.ds(r0, tq)],
                   kprev_ref, vprev_ref, proj_ref, att_ref)
        return carry

    lax.fori_loop(0, n_sub, sub_tile, 0)


def _attn_tile(t, sinks_ref, x_ref, pos_ref, invf_ref, spread_ref, prew_ref, postw_ref, win_ref, bin_ref,
               wout_ref, bout_ref, o_ref, kprev_ref, vprev_ref, proj_ref, att_ref):
    tq = x_ref.shape[0]
    n_blocks = tq // ATTN_BLOCK

    @pl.when(t == 0)
    def _():
        kprev_ref[...] = jnp.zeros_like(kprev_ref)
        vprev_ref[...] = jnp.zeros_like(vprev_ref)

    x = x_ref[...]
    h = _rmsnorm(x, prew_ref[...]).astype(BF16)

    def project(c0, c1):
        proj_ref[:, c0:c1] = _dot(h, win_ref[:, c0:c1]) + bin_ref[:, c0:c1]

    q_cols = GROUP * HEAD_DIM
    z0 = ATTN_WIDTH + 2 * KV_WIDTH
    project(ATTN_WIDTH, z0)
    project(0, q_cols)
    later_chunks = [(g * q_cols, (g + 1) * q_cols) for g in range(1, N_KV_HEADS)]
    later_chunks += [(z0, z0 + ATTN_WIDTH // 2), (z0 + ATTN_WIDTH // 2, z0 + ATTN_WIDTH)]

    ang_t = invf_ref[...] * pos_ref[...].astype(F32)
    trig_t = jnp.concatenate([jnp.cos(ang_t), jnp.sin(ang_t)], axis=0)
    tabs = _split_dot_tn(trig_t, spread_ref[...])
    lane = lax.broadcasted_iota(jnp.int32, (1, LANES), 1)
    d = lane % HEAD_DIM
    c_tab = tabs[:, :LANES] + jnp.where(d < ROPE_DIM, 0.0, 1.0)
    s_lo = tabs[:, LANES:2 * LANES]
    s_hi = tabs[:, 2 * LANES:]

    def rope(xc, rows=slice(None)):
        return (xc * c_tab[rows] + pltpu.roll(xc, LANES - ROPE_HALF, 1) * s_lo[rows]
                + pltpu.roll(xc, ROPE_HALF, 1) * s_hi[rows])

    k_all = rope(proj_ref[:, ATTN_WIDTH:ATTN_WIDTH + KV_WIDTH])
    v_all = proj_ref[:, ATTN_WIDTH + KV_WIDTH:ATTN_WIDTH + 2 * KV_WIDTH]

    lo_half = lane < HEAD_DIM
    from_prev = (lax.broadcasted_iota(jnp.int32, (ATTN_BLOCK, ATTN_BLOCK), 1)
                 > lax.broadcasted_iota(jnp.int32, (ATTN_BLOCK, ATTN_BLOCK), 0))
    no_prev = jnp.where(t == 0, -jnp.inf, 0.0)
    scale = HEAD_DIM ** -0.5 * LOG2_E

    def block_diag(a):
        a_sw = pltpu.roll(a, HEAD_DIM, 1)
        zero = jnp.zeros_like(a)
        g0 = jnp.concatenate([jnp.where(lo_half, a, zero), jnp.where(lo_half, zero, a_sw)], axis=0)
        g1 = jnp.concatenate([jnp.where(lo_half, a_sw, zero), jnp.where(lo_half, zero, a)], axis=0)
        return g0.astype(BF16), g1.astype(BF16)

    k_prev = kprev_ref[...]
    v_prev = vprev_ref[...]
    kbs, vbs = [], []
    for j in range(n_blocks):
        rows = slice(j * ATTN_BLOCK, (j + 1) * ATTN_BLOCK)
        k_cur, v_cur = k_all[rows], v_all[rows]
        kbs.append(block_diag(jnp.concatenate([k_prev, k_cur], axis=0)))
        vbs.append(block_diag(jnp.concatenate([v_prev, v_cur], axis=0)))
        k_prev, v_prev = k_cur, v_cur
    kprev_ref[...] = k_prev
    vprev_ref[...] = v_prev

    pairs_per_group = GROUP // 2

    def scores(j, g):
        rows = slice(j * ATTN_BLOCK, (j + 1) * ATTN_BLOCK)
        qs = [(rope(proj_ref[rows, p * LANES:(p + 1) * LANES], rows) * scale).astype(BF16)
              for p in range(g * pairs_per_group, (g + 1) * pairs_per_group)]
        return _dot_nt(jnp.concatenate(qs, axis=0), kbs[j][g])

    items = [(j, g) for g in range(N_KV_HEADS) for j in range(n_blocks)]
    s_next = scores(*items[0])
    for i, (j, g) in enumerate(items):
        s_group = s_next
        if later_chunks:
            project(*later_chunks.pop(0))
        if i + 1 < len(items):
            s_next = scores(*items[i + 1])
        probs, invs = [], []
        for pi in range(pairs_per_group):
            p = g * pairs_per_group + pi
            s = s_group[pi * ATTN_BLOCK:(pi + 1) * ATTN_BLOCK]
            halves, inv = [], []
            for e in range(2):
                s_prev = s[:, 2 * e * ATTN_BLOCK:(2 * e + 1) * ATTN_BLOCK]
                s_cur = s[:, (2 * e + 1) * ATTN_BLOCK:(2 * e + 2) * ATTN_BLOCK]
                if j == 0:
                    s_prev = s_prev + no_prev
                sw = jnp.where(from_prev, s_prev, s_cur)
                sink = sinks_ref[2 * p + e] * LOG2_E
                m = jnp.maximum(jnp.max(sw, axis=-1, keepdims=True), sink)
                pw = jnp.exp2(sw - m)
                denom = jnp.sum(pw, axis=-1, keepdims=True) + jnp.exp2(sink - m)
                pw = pw.astype(BF16)
                zero = jnp.zeros_like(pw)
                halves += [jnp.where(from_prev, pw, zero), jnp.where(from_prev, zero, pw)]
                inv.append(1.0 / denom)
            probs.append(jnp.concatenate(halves, axis=1))
            invs.append(jnp.where(lo_half, inv[0], inv[1]))
        o = _dot(jnp.concatenate(probs, axis=0), vbs[j][g])
        for pi in range(pairs_per_group):
            p = g * pairs_per_group + pi
            att_ref[j * ATTN_BLOCK:(j + 1) * ATTN_BLOCK, p * LANES:(p + 1) * LANES] = (
                o[pi * ATTN_BLOCK:(pi + 1) * ATTN_BLOCK] * invs[pi])
    for chunk in later_chunks:
        project(*chunk)

    z = proj_ref[:, z0:]
    gated = (att_ref[...] * (z * _sigmoid(z))).astype(BF16)
    y = _dot(gated, wout_ref[...]) + bout_ref[...]
    o_ref[...] = x + _rmsnorm(y, postw_ref[...])


def _rope_spread_matrix():
    p = np.zeros((2 * ROPE_HALF, 3 * LANES), np.float32)
    for lane in range(LANES):
        d = lane % HEAD_DIM
        if d < ROPE_DIM:
            p[d % ROPE_HALF, lane] = 1.0
        if d < ROPE_HALF:
            p[ROPE_HALF + d, LANES + lane] = -1.0
        elif d < ROPE_DIM:
            p[ROPE_HALF + d - ROPE_HALF, 2 * LANES + lane] = 1.0
    return jnp.asarray(p, dtype=BF16)


def _attn_layer(x, positions, pre_w, post_w, w_in, b_in, sinks, w_out, b_out):
    B, T, D = x.shape
    tq = min(ATTN_TILE, T)
    step = _step_rows(T, tq)
    assert T % tq == 0 and tq % ATTN_BLOCK == 0 and D == D_MODEL
    inv_freq = ROPE_THETA ** (-(jnp.arange(ROPE_HALF, dtype=F32) * 2.0 / ROPE_DIM))
    const = lambda b, t: (0, 0)
    resident = pl.Buffered(1)
    return pl.pallas_call(
        functools.partial(_attn_layer_kernel, tq),
        out_shape=jax.ShapeDtypeStruct((B, T, D), x.dtype),
        grid=(B, T // step),
        in_specs=[
            pl.BlockSpec(memory_space=pltpu.SMEM),
            pl.BlockSpec((None, step, D), lambda b, t: (b, t, 0)),
            pl.BlockSpec((None, 1, step), lambda b, t: (b, 0, t)),
            pl.BlockSpec((ROPE_HALF, 1), const),
            pl.BlockSpec((2 * ROPE_HALF, 3 * LANES), const),
            pl.BlockSpec((1, D), const),
            pl.BlockSpec((1, D), const),
            pl.BlockSpec((D, ATTN_IN), const, pipeline_mode=resident),
            pl.BlockSpec((1, ATTN_IN), const),
            pl.BlockSpec((ATTN_WIDTH, D), const, pipeline_mode=resident),
            pl.BlockSpec((1, D), const),
        ],
        out_specs=pl.BlockSpec((None, step, D), lambda b, t: (b, t, 0)),
        scratch_shapes=[
            pltpu.VMEM((ATTN_BLOCK, KV_WIDTH), F32),
            pltpu.VMEM((ATTN_BLOCK, KV_WIDTH), F32),
            pltpu.VMEM((tq, ATTN_IN), F32),
            pltpu.VMEM((tq, ATTN_WIDTH), F32),
        ],
        compiler_params=pltpu.CompilerParams(
            dimension_semantics=("arbitrary", "arbitrary"),
            vmem_limit_bytes=VMEM_LIMIT_BYTES),
        name="swa_layer",
    )(sinks.astype(F32), x, positions.reshape(B, 1, T), inv_freq.reshape(ROPE_HALF, 1), _rope_spread_matrix(),
      pre_w.reshape(1, D), post_w.reshape(1, D),
      w_in.astype(BF16), b_in.reshape(1, ATTN_IN), w_out.astype(BF16), b_out.reshape(1, D))


def _boundary_rows(b, m):
    c, n = b.shape
    if 2 * m >= SUBLANES:
        b3 = b.reshape(c // (2 * m), 2 * m, n)
        return jnp.broadcast_to(b3[:, m - 1:m, :], b3.shape).reshape(c, n)
    b3 = b.reshape(c // SUBLANES, SUBLANES, n)
    sub = lax.broadcasted_iota(jnp.int32, b3.shape, 1)
    r = jnp.broadcast_to(b3[:, SUBLANES - m - 1:SUBLANES - m, :], b3.shape)
    for start in range(SUBLANES - 4 * m, -1, -2 * m):
        r = jnp.where(sub < start + 2 * m, jnp.broadcast_to(b3[:, start + m - 1:start + m, :], b3.shape), r)
    return r.reshape(c, n)


def _rec_layer_kernel(layer, chunk, x_ref, *rest):
    *consts, o_ref, state_ref, proj_ref, q_ref, k_ref, ghi_ref, glo_ref, b_ref, gated_ref = rest
    n_sub = x_ref.shape[0] // chunk

    def sub_chunk(i, carry):
        r0 = pl.multiple_of(i * chunk, chunk)
        _rec_chunk(pl.program_id(1) * n_sub + i, layer, x_ref.at[pl.ds(r0, chunk)], *consts,
                   o_ref.at[pl.ds(r0, chunk)], state_ref, proj_ref, q_ref, k_ref, ghi_ref, glo_ref, b_ref,
                   gated_ref)
        return carry

    lax.fori_loop(0, n_sub, sub_chunk, 0)


def _rec_chunk(c, layer, x_ref, lbl_ref, lv_ref, ltri_ref, prew_ref, postw_ref, win_ref, gnw_ref,
               wout_ref, o_ref, state_ref, proj_ref, q_ref, k_ref, ghi_ref, glo_ref, b_ref, gated_ref):
    chunk = x_ref.shape[0]
    seg = lv_ref.shape[0]
    seg_levels = seg.bit_length() - 1
    base_levels = REC_BASE.bit_length() - 1

    @pl.when(c == 0)
    def _():
        state_ref[...] = jnp.zeros_like(state_ref)

    x = x_ref[...]
    h = _rmsnorm(x, prew_ref[...]).astype(BF16)

    def project(c0, c1):
        proj_ref[:, c0:c1] = _dot(h, win_ref[:, c0:c1])

    project(FORGET_DIM, 2 * FORGET_DIM)
    project(0, FORGET_DIM)

    def forget_gates(hd):
        off = hd * LANES
        f_raw = proj_ref[:, FORGET_DIM + off:FORGET_DIM + off + LANES]

        logits = lbl_ref[:, off:off + LANES]
        ex = jnp.exp(logits - jnp.max(logits, axis=0, keepdims=True))
        lb = jnp.sum(ex[1:layer + 1], axis=0, keepdims=True) / jnp.sum(ex, axis=0, keepdims=True)

        e = jnp.exp(-jnp.abs(f_raw))
        sp = 1.0 / (1.0 + e)
        pos = f_raw >= 0
        sig = jnp.where(pos, sp, e * sp)
        nsig = jnp.where(pos, e * sp, sp)
        log2_f = jnp.log2(lb + (1.0 - lb) * sig)
        k_ref[:, off:off + LANES] = ((1.0 - lb) * nsig).astype(BF16)
        g_hi = log2_f.astype(BF16)
        ghi_ref[:, off:off + LANES] = g_hi
        glo_ref[:, off:off + LANES] = (log2_f - g_hi.astype(F32)).astype(BF16)

    for hd in range(REC_HEADS):
        forget_gates(hd)
    project(2 * FORGET_DIM, 2 * FORGET_DIM + REC_WIDTH)
    project(2 * FORGET_DIM + REC_WIDTH, REC_IN)

    ltri = ltri_ref[...]
    b_all = _dot(ltri, ghi_ref[...]) + _dot(ltri, glo_ref[...])
    for hd in range(REC_HEADS):
        b_ref[hd] = b_all[:, hd * LANES:(hd + 1) * LANES]
        q_raw = proj_ref[:, hd * LANES:(hd + 1) * LANES]
        q_ref[:, hd * LANES:(hd + 1) * LANES] = (q_raw * _sigmoid(q_raw)).astype(BF16)

    def head(hd, bounded):
        off = hd * LANES
        b = b_ref[hd]
        q = q_ref[:, off:off + LANES]
        k = k_ref[:, off:off + LANES]
        v = proj_ref[:, 2 * FORGET_DIM + off:2 * FORGET_DIM + off + LANES].astype(BF16)
        lv = lv_ref[...]

        outs = []
        for s0 in range(0, chunk, seg):
            rows = slice(s0, s0 + seg)
            qs, ks, bs = q[rows], k[rows], b[rows]
            if bounded:
                b3 = bs.reshape(seg // REC_BASE, REC_BASE, LANES)
                r = jnp.broadcast_to(b3[:, 0:1, :], b3.shape).reshape(seg, LANES)
                a = jnp.where(lv < base_levels,
                              _dot_nt(qs * jnp.exp2(bs - r).astype(BF16), ks * jnp.exp2(r - bs).astype(BF16)), 0.0)
                first_level = base_levels
            else:
                a = jnp.where(lv == -1, _dot_nt(qs, ks), 0.0)
                first_level = 0
            for j in range(first_level, seg_levels):
                en = jnp.exp2(-jnp.abs(bs - _boundary_rows(bs, 1 << j))).astype(BF16)
                a = jnp.where(lv == j, _dot_nt(qs * en, ks * en), a)
            outs.append(_dot(a.astype(BF16), v[rows]))
        m = seg
        while m < chunk:
            for base in range(0, chunk, 2 * m):
                lo, hi = slice(base, base + m), slice(base + m, base + 2 * m)
                r = b[base + m - 1:base + m]
                qh = q[hi] * jnp.exp2(b[hi] - r).astype(BF16)
                kh = k[lo] * jnp.exp2(r - b[lo]).astype(BF16)
                upd = _dot(_dot_nt(qh, kh).astype(BF16), v[lo])
                for i in range(m // seg):
                    outs[(base + m) // seg + i] += upd[i * seg:(i + 1) * seg]
            m *= 2
        o = jnp.concatenate(outs, axis=0)

        b_last = b[chunk - 1:chunk, :]
        st = state_ref[hd]
        o = o + _dot_nt(q * jnp.exp2(b).astype(BF16), st.astype(BF16))
        kd = k * jnp.exp2(b_last - b).astype(BF16)
        state_ref[hd] = st * jnp.exp2(b_last) + _dot_tn(v, kd)

        o = o * lax.rsqrt(jnp.mean(o * o, axis=-1, keepdims=True) + NORM_EPS) * gnw_ref[...]
        z_off = 2 * FORGET_DIM + REC_WIDTH + off
        z = proj_ref[:, z_off:z_off + LANES]
        gated_ref[:, off:off + LANES] = (o * (z * _sigmoid(z))).astype(BF16)

    def all_heads(bounded):
        for hd in range(REC_HEADS):
            head(hd, bounded)

    n_base = chunk // REC_BASE
    spread = jnp.max(b_ref[:, pl.ds(0, n_base, stride=REC_BASE), :]
                     - b_ref[:, pl.ds(REC_BASE - 1, n_base, stride=REC_BASE), :])
    lax.cond(spread <= SAFE_LOG2_SPREAD, lambda: all_heads(True), lambda: all_heads(False))

    y = _dot(gated_ref[...], wout_ref[...])
    o_ref[...] = x + _rmsnorm(y, postw_ref[...])


def _level_index(c):
    t = np.arange(c)[:, None]
    s = np.arange(c)[None, :]
    lv = np.floor(np.log2(np.maximum(t ^ s, 1))).astype(np.int32)
    lv = np.where(s < t, lv, np.where(s == t, -1, 99))
    return jnp.asarray(lv, dtype=jnp.int32)


def _rec_layer(x, layer, lb_logits, pre_w, post_w, w_in, gnorm_w, w_out):
    B, T, D = x.shape
    chunk = min(REC_CHUNK, T)
    seg = min(REC_SEG, chunk)
    assert T % chunk == 0 and chunk & (chunk - 1) == 0 and seg >= 2 * SUBLANES and D == D_MODEL
    depth = lb_logits.shape[0]
    ltri = jnp.asarray(np.tril(np.ones((chunk, chunk), np.float32)), dtype=BF16)
    const = lambda b, t: (0, 0)
    step = _step_rows(T, chunk)
    resident = pl.Buffered(1)
    return pl.pallas_call(
        functools.partial(_rec_layer_kernel, layer, chunk),
        out_shape=jax.ShapeDtypeStruct((B, T, D), x.dtype),
        grid=(B, T // step),
        in_specs=[
            pl.BlockSpec((None, step, D), lambda b, t: (b, t, 0)),
            pl.BlockSpec((depth, FORGET_DIM), const),
            pl.BlockSpec((seg, seg), const),
            pl.BlockSpec((chunk, chunk), const),
            pl.BlockSpec((1, D), const),
            pl.BlockSpec((1, D), const),
            pl.BlockSpec((D, REC_IN), const, pipeline_mode=resident),
            pl.BlockSpec((1, REC_VALUE_DIM), const),
            pl.BlockSpec((REC_WIDTH, D), const, pipeline_mode=resident),
        ],
        out_specs=pl.BlockSpec((None, step, D), lambda b, t: (b, t, 0)),
        scratch_shapes=[
            pltpu.VMEM((REC_HEADS, REC_VALUE_DIM, REC_KEY_DIM), F32),
            pltpu.VMEM((chunk, REC_IN), F32),
            pltpu.VMEM((chunk, FORGET_DIM), BF16),
            pltpu.VMEM((chunk, FORGET_DIM), BF16),
            pltpu.VMEM((chunk, FORGET_DIM), BF16),
            pltpu.VMEM((chunk, FORGET_DIM), BF16),
            pltpu.VMEM((REC_HEADS, chunk, REC_KEY_DIM), F32),
            pltpu.VMEM((chunk, REC_WIDTH), BF16),
        ],
        compiler_params=pltpu.CompilerParams(
            dimension_semantics=("arbitrary", "arbitrary"),
            vmem_limit_bytes=VMEM_LIMIT_BYTES),
        name="hgrn2_layer",
    )(x, lb_logits.astype(F32), _level_index(seg), ltri, pre_w.reshape(1, D), post_w.reshape(1, D),
      w_in.astype(BF16), gnorm_w.reshape(1, REC_VALUE_DIM), w_out.astype(BF16))


def kernel(x, positions, pre_norm_w, post_norm_w, attn_w_in, attn_b_in, attn_sinks, attn_w_out, attn_b_out,
           rec_w_in, rec_lb_logits, rec_gnorm_w, rec_w_out):
    depth = pre_norm_w.shape[0]
    for layer in range(depth):
        j = layer // N_MIXERS
        if layer % N_MIXERS == 0:
            x = _attn_layer(x, positions, pre_norm_w[layer], post_norm_w[layer], attn_w_in[j], attn_b_in[j],
                            attn_sinks[j], attn_w_out[j], attn_b_out[j])
        else:
            x = _rec_layer(x, layer, rec_lb_logits, pre_norm_w[layer], post_norm_w[layer], rec_w_in[j],
                           rec_gnorm_w[j], rec_w_out[j])
    return x
```

```python
import functools

import numpy as np
import jax
import jax.numpy as jnp
from jax import lax
from jax.experimental import pallas as pl
from jax.experimental.pallas import tpu as pltpu

D_MODEL = 1024
N_MIXERS = 2

HEAD_DIM = 64
N_HEADS = D_MODEL // HEAD_DIM
N_KV_HEADS = 2
GROUP = N_HEADS // N_KV_HEADS
ATTN_WIDTH = N_HEADS * HEAD_DIM
KV_WIDTH = N_KV_HEADS * HEAD_DIM
ATTN_IN = 2 * ATTN_WIDTH + 2 * KV_WIDTH
ATTN_BLOCK = 128
ROPE_THETA = 500000.0
ROPE_DIM = HEAD_DIM // 4
ROPE_HALF = ROPE_DIM // 2

REC_HEADS = 8
REC_KEY_DIM = 128
REC_VALUE_DIM = D_MODEL // REC_HEADS
FORGET_DIM = REC_HEADS * REC_KEY_DIM
REC_WIDTH = REC_HEADS * REC_VALUE_DIM
REC_IN = 2 * FORGET_DIM + 2 * REC_WIDTH

NORM_EPS = 1e-6
LOG2_E = 1.4426950408889634

LANES = 128
SUBLANES = 8
ATTN_TILE = 256
REC_CHUNK = 256
REC_SEG = 64
REC_BASE = 32
SAFE_LOG2_SPREAD = 96.0
VMEM_LIMIT_BYTES = 48 * 1024 * 1024

F32 = jnp.float32
BF16 = jnp.bfloat16


def _staggered_tiles(batch, tiles_per_seq):
    n = batch * tiles_per_seq

    def cur(s):
        c = jnp.minimum(s, n - 1)
        return c // tiles_per_seq, c % tiles_per_seq

    def prev(s):
        p = jnp.maximum(s - 1, 0)
        return p // tiles_per_seq, p % tiles_per_seq

    return n, cur, prev


def _rmsnorm(x, w):
    return x * lax.rsqrt(jnp.mean(x * x, axis=-1, keepdims=True) + NORM_EPS) * w


def _sigmoid(x):
    return 1.0 / (1.0 + jnp.exp(-x))


def _dot(a, b):
    return jnp.dot(a, b, preferred_element_type=F32)


def _dot_nt(a, b):
    return lax.dot_general(a, b, (((1,), (1,)), ((), ())), preferred_element_type=F32)


def _dot_tn(a, b):
    return lax.dot_general(a, b, (((0,), (0,)), ((), ())), preferred_element_type=F32)


def _split_dot_tn(x, p):
    acc = None
    for _ in range(3):
        part = x.astype(BF16)
        x = x - part.astype(F32)
        term = _dot_tn(part, p)
        acc = term if acc is None else acc + term
    return acc


def _attn_layer_kernel(tiles_per_seq, n_tiles, sinks_ref, x_ref, pos_ref, xprev_ref, invf_ref, spread_ref,
                       prew_ref, postw_ref, win_ref, bin_ref, wout_ref, bout_ref, o_ref,
                       kprev_ref, vprev_ref, proj_ref, att_ref, gated_ref):
    s = pl.program_id(0)
    t = jnp.minimum(s, n_tiles - 1) % tiles_per_seq
    tq = x_ref.shape[0]
    n_blocks = tq // ATTN_BLOCK

    @pl.when(s == 0)
    def _():
        gated_ref[...] = jnp.zeros_like(gated_ref)

    @pl.when(t == 0)
    def _():
        kprev_ref[...] = jnp.zeros_like(kprev_ref)
        vprev_ref[...] = jnp.zeros_like(vprev_ref)

    y_prev = _dot(gated_ref[...], wout_ref[...]) + bout_ref[...]

    ang_t = invf_ref[...] * pos_ref[...].astype(F32)
    trig_t = jnp.concatenate([jnp.cos(ang_t), jnp.sin(ang_t)], axis=0)
    tabs = _split_dot_tn(trig_t, spread_ref[...])
    lane = lax.broadcasted_iota(jnp.int32, (1, LANES), 1)
    d = lane % HEAD_DIM
    c_tab = tabs[:, :LANES] + jnp.where(d < ROPE_DIM, 0.0, 1.0)
    s_lo = tabs[:, LANES:2 * LANES]
    s_hi = tabs[:, 2 * LANES:]

    x = x_ref[...]
    h = _rmsnorm(x, prew_ref[...]).astype(BF16)

    def project(c0, c1):
        proj_ref[:, c0:c1] = _dot(h, win_ref[:, c0:c1]) + bin_ref[:, c0:c1]

    q_cols = GROUP * HEAD_DIM
    z0 = ATTN_WIDTH + 2 * KV_WIDTH
    project(ATTN_WIDTH, z0)
    project(0, q_cols)
    project(q_cols, 2 * q_cols)
    later_chunks = [(g * q_cols, (g + 1) * q_cols) for g in range(2, N_KV_HEADS)]
    later_chunks += [(z0 + i * ATTN_WIDTH // 4, z0 + (i + 1) * ATTN_WIDTH // 4) for i in range(4)]

    o_ref[...] = xprev_ref[...] + _rmsnorm(y_prev, postw_ref[...])

    def rope(xc, rows=slice(None)):
        return (xc * c_tab[rows] + pltpu.roll(xc, LANES - ROPE_HALF, 1) * s_lo[rows]
                + pltpu.roll(xc, ROPE_HALF, 1) * s_hi[rows])

    k_all = rope(proj_ref[:, ATTN_WIDTH:ATTN_WIDTH + KV_WIDTH])
    v_all = proj_ref[:, ATTN_WIDTH + KV_WIDTH:ATTN_WIDTH + 2 * KV_WIDTH]

    lo_half = lane < HEAD_DIM
    from_prev = (lax.broadcasted_iota(jnp.int32, (ATTN_BLOCK, ATTN_BLOCK), 1)
                 > lax.broadcasted_iota(jnp.int32, (ATTN_BLOCK, ATTN_BLOCK), 0))
    no_prev = jnp.where(t == 0, -jnp.inf, 0.0)
    scale = HEAD_DIM ** -0.5 * LOG2_E

    def block_diag(a):
        a_sw = pltpu.roll(a, HEAD_DIM, 1)
        zero = jnp.zeros_like(a)
        g0 = jnp.concatenate([jnp.where(lo_half, a, zero), jnp.where(lo_half, zero, a_sw)], axis=0)
        g1 = jnp.concatenate([jnp.where(lo_half, a_sw, zero), jnp.where(lo_half, zero, a)], axis=0)
        return g0.astype(BF16), g1.astype(BF16)

    k_prev = kprev_ref[...]
    v_prev = vprev_ref[...]
    kbs, vbs = [], []
    for j in range(n_blocks):
        rows = slice(j * ATTN_BLOCK, (j + 1) * ATTN_BLOCK)
        k_cur, v_cur = k_all[rows], v_all[rows]
        kbs.append(block_diag(jnp.concatenate([k_prev, k_cur], axis=0)))
        vbs.append(block_diag(jnp.concatenate([v_prev, v_cur], axis=0)))
        k_prev, v_prev = k_cur, v_cur
    kprev_ref[...] = k_prev
    vprev_ref[...] = v_prev

    pairs_per_group = GROUP // 2

    def scores(j, g):
        rows = slice(j * ATTN_BLOCK, (j + 1) * ATTN_BLOCK)
        qs = [(rope(proj_ref[rows, p * LANES:(p + 1) * LANES], rows) * scale).astype(BF16)
              for p in range(g * pairs_per_group, (g + 1) * pairs_per_group)]
        return _dot_nt(jnp.concatenate(qs, axis=0), kbs[j][g])

    items = [(j, g) for g in range(N_KV_HEADS) for j in range(n_blocks)]
    s_next = scores(*items[0])
    for i, (j, g) in enumerate(items):
        s_group = s_next
        if later_chunks:
            project(*later_chunks.pop(0))
        if i + 1 < len(items):
            s_next = scores(*items[i + 1])
        probs, invs = [], []
        for pi in range(pairs_per_group):
            p = g * pairs_per_group + pi
            s = s_group[pi * ATTN_BLOCK:(pi + 1) * ATTN_BLOCK]
            halves, inv = [], []
            for e in range(2):
                s_prev = s[:, 2 * e * ATTN_BLOCK:(2 * e + 1) * ATTN_BLOCK]
                s_cur = s[:, (2 * e + 1) * ATTN_BLOCK:(2 * e + 2) * ATTN_BLOCK]
                if j == 0:
                    s_prev = s_prev + no_prev
                sw = jnp.where(from_prev, s_prev, s_cur)
                sink = sinks_ref[2 * p + e] * LOG2_E
                m = jnp.maximum(jnp.max(sw, axis=-1, keepdims=True), sink)
                pw = jnp.exp2(sw - m)
                denom = jnp.sum(pw, axis=-1, keepdims=True) + jnp.exp2(sink - m)
                pw = pw.astype(BF16)
                zero = jnp.zeros_like(pw)
                halves += [jnp.where(from_prev, pw, zero), jnp.where(from_prev, zero, pw)]
                inv.append(1.0 / denom)
            probs.append(jnp.concatenate(halves, axis=1))
            invs.append(jnp.where(lo_half, inv[0], inv[1]))
        o = _dot(jnp.concatenate(probs, axis=0), vbs[j][g])
        for pi in range(pairs_per_group):
            p = g * pairs_per_group + pi
            att_ref[j * ATTN_BLOCK:(j + 1) * ATTN_BLOCK, p * LANES:(p + 1) * LANES] = (
                o[pi * ATTN_BLOCK:(pi + 1) * ATTN_BLOCK] * invs[pi])
    for chunk in later_chunks:
        project(*chunk)

    z = proj_ref[:, z0:]
    gated_ref[...] = (att_ref[...] * (z * _sigmoid(z))).astype(BF16)


def _rope_spread_matrix():
    p = np.zeros((2 * ROPE_HALF, 3 * LANES), np.float32)
    for lane in range(LANES):
        d = lane % HEAD_DIM
        if d < ROPE_DIM:
            p[d % ROPE_HALF, lane] = 1.0
        if d < ROPE_HALF:
            p[ROPE_HALF + d, LANES + lane] = -1.0
        elif d < ROPE_DIM:
            p[ROPE_HALF + d - ROPE_HALF, 2 * LANES + lane] = 1.0
    return jnp.asarray(p, dtype=BF16)


def _attn_layer(x, positions, pre_w, post_w, w_in, b_in, sinks, w_out, b_out):
    B, T, D = x.shape
    tq = min(ATTN_TILE, T)
    assert T % tq == 0 and tq % ATTN_BLOCK == 0 and D == D_MODEL
    inv_freq = ROPE_THETA ** (-(jnp.arange(ROPE_HALF, dtype=F32) * 2.0 / ROPE_DIM))
    const = lambda s: (0, 0)
    nt = T // tq
    n_tiles, cur, prev = _staggered_tiles(B, nt)
    return pl.pallas_call(
        functools.partial(_attn_layer_kernel, nt, n_tiles),
        out_shape=jax.ShapeDtypeStruct((B, T, D), x.dtype),
        grid=(n_tiles + 1,),
        in_specs=[
            pl.BlockSpec(memory_space=pltpu.SMEM),
            pl.BlockSpec((None, tq, D), lambda s: (*cur(s), 0)),
            pl.BlockSpec((None, 1, tq), lambda s: (cur(s)[0], 0, cur(s)[1])),
            pl.BlockSpec((None, tq, D), lambda s: (*prev(s), 0)),
            pl.BlockSpec((ROPE_HALF, 1), const),
            pl.BlockSpec((2 * ROPE_HALF, 3 * LANES), const),
            pl.BlockSpec((1, D), const),
            pl.BlockSpec((1, D), const),
            pl.BlockSpec((D, ATTN_IN), const),
            pl.BlockSpec((1, ATTN_IN), const),
            pl.BlockSpec((ATTN_WIDTH, D), const),
            pl.BlockSpec((1, D), const),
        ],
        out_specs=pl.BlockSpec((None, tq, D), lambda s: (*prev(s), 0)),
        scratch_shapes=[
            pltpu.VMEM((ATTN_BLOCK, KV_WIDTH), F32),
            pltpu.VMEM((ATTN_BLOCK, KV_WIDTH), F32),
            pltpu.VMEM((tq, ATTN_IN), F32),
            pltpu.VMEM((tq, ATTN_WIDTH), F32),
            pltpu.VMEM((tq, ATTN_WIDTH), BF16),
        ],
        compiler_params=pltpu.CompilerParams(
            dimension_semantics=("arbitrary",),
            vmem_limit_bytes=VMEM_LIMIT_BYTES),
        name="swa_layer",
    )(sinks.astype(F32), x, positions.reshape(B, 1, T), x, inv_freq.reshape(ROPE_HALF, 1), _rope_spread_matrix(),
      pre_w.reshape(1, D), post_w.reshape(1, D),
      w_in.astype(BF16), b_in.reshape(1, ATTN_IN), w_out.astype(BF16), b_out.reshape(1, D))


def _boundary_rows(b, m):
    c, n = b.shape
    if 2 * m >= SUBLANES:
        b3 = b.reshape(c // (2 * m), 2 * m, n)
        return jnp.broadcast_to(b3[:, m - 1:m, :], b3.shape).reshape(c, n)
    b3 = b.reshape(c // SUBLANES, SUBLANES, n)
    sub = lax.broadcasted_iota(jnp.int32, b3.shape, 1)
    r = jnp.broadcast_to(b3[:, SUBLANES - m - 1:SUBLANES - m, :], b3.shape)
    for start in range(SUBLANES - 4 * m, -1, -2 * m):
        r = jnp.where(sub < start + 2 * m, jnp.broadcast_to(b3[:, start + m - 1:start + m, :], b3.shape), r)
    return r.reshape(c, n)


def _rec_layer_kernel(layer, x_ref, lbl_ref, lv_ref, ltri_ref, bsum_ref, prew_ref, postw_ref, win_ref, gnw_ref,
                      wout_ref, o_ref, state_ref, proj_ref, q_ref, k_ref, ghi_ref, glo_ref, b_ref,
                      gated_ref):
    c = pl.program_id(1)
    chunk = x_ref.shape[0]
    seg = lv_ref.shape[0]
    seg_levels = seg.bit_length() - 1
    base_levels = REC_BASE.bit_length() - 1

    @pl.when(c == 0)
    def _():
        state_ref[...] = jnp.zeros_like(state_ref)

    x = x_ref[...]
    h = _rmsnorm(x, prew_ref[...]).astype(BF16)

    def project(c0, c1):
        proj_ref[:, c0:c1] = _dot(h, win_ref[:, c0:c1])

    project(FORGET_DIM, 2 * FORGET_DIM)
    project(0, FORGET_DIM)
    project(2 * FORGET_DIM, 2 * FORGET_DIM + REC_WIDTH)

    def forget_gates(hd):
        off = hd * LANES
        f_raw = proj_ref[:, FORGET_DIM + off:FORGET_DIM + off + LANES]

        logits = lbl_ref[:, off:off + LANES]
        ex = jnp.exp(logits - jnp.max(logits, axis=0, keepdims=True))
        lb = jnp.sum(ex[1:layer + 1], axis=0, keepdims=True) / jnp.sum(ex, axis=0, keepdims=True)

        e = jnp.exp(-jnp.abs(f_raw))
        sp = 1.0 / (1.0 + e)
        pos = f_raw >= 0
        sig = jnp.where(pos, sp, e * sp)
        nsig = jnp.where(pos, e * sp, sp)
        log2_f = jnp.log2(lb + (1.0 - lb) * sig)
        k_ref[:, off:off + LANES] = ((1.0 - lb) * nsig).astype(BF16)
        g_hi = log2_f.astype(BF16)
        ghi_ref[:, off:off + LANES] = g_hi
        glo_ref[:, off:off + LANES] = (log2_f - g_hi.astype(F32)).astype(BF16)

    for hd in range(REC_HEADS):
        forget_gates(hd)

    bsum = bsum_ref[...]
    spread = -jnp.min(_dot(bsum, ghi_ref[...]) + _dot(bsum, glo_ref[...]))

    ltri = ltri_ref[...]
    b_all = _dot(ltri, ghi_ref[...]) + _dot(ltri, glo_ref[...])
    for hd in range(REC_HEADS):
        b_ref[hd] = b_all[:, hd * LANES:(hd + 1) * LANES]
        q_raw = proj_ref[:, hd * LANES:(hd + 1) * LANES]
        q_ref[:, hd * LANES:(hd + 1) * LANES] = (q_raw * _sigmoid(q_raw)).astype(BF16)

    def head_scores(hd, bounded):
        off = hd * LANES
        b = b_ref[hd]
        q = q_ref[:, off:off + LANES]
        k = k_ref[:, off:off + LANES]
        lv = lv_ref[...]

        seg_scores = []
        for s0 in range(0, chunk, seg):
            rows = slice(s0, s0 + seg)
            qs, ks, bs = q[rows], k[rows], b[rows]
            if bounded:
                b3 = bs.reshape(seg // REC_BASE, REC_BASE, LANES)
                r = jnp.broadcast_to(b3[:, 0:1, :], b3.shape).reshape(seg, LANES)
                a = jnp.where(lv < base_levels,
                              _dot_nt(qs * jnp.exp2(bs - r).astype(BF16), ks * jnp.exp2(r - bs).astype(BF16)), 0.0)
                first_level = base_levels
            else:
                a = jnp.where(lv == -1, _dot_nt(qs, ks), 0.0)
                first_level = 0
            for j in range(first_level, seg_levels):
                en = jnp.exp2(-jnp.abs(bs - _boundary_rows(bs, 1 << j))).astype(BF16)
                a = jnp.where(lv == j, _dot_nt(qs * en, ks * en), a)
            seg_scores.append(a.astype(BF16))
        block_scores = []
        m = seg
        while m < chunk:
            for base in range(0, chunk, 2 * m):
                lo, hi = slice(base, base + m), slice(base + m, base + 2 * m)
                r = b[base + m - 1:base + m]
                qh = q[hi] * jnp.exp2(b[hi] - r).astype(BF16)
                kh = k[lo] * jnp.exp2(r - b[lo]).astype(BF16)
                block_scores.append((base, m, _dot_nt(qh, kh).astype(BF16)))
            m *= 2
        b_last = b[chunk - 1:chunk, :]
        st = state_ref[hd]
        o_inter = _dot_nt(q * jnp.exp2(b).astype(BF16), st.astype(BF16))
        kd = k * jnp.exp2(b_last - b).astype(BF16)
        return seg_scores, block_scores, o_inter, kd, st * jnp.exp2(b_last)

    def head_finish(hd, scores):
        seg_scores, block_scores, o_inter, kd, st_decayed = scores
        off = 2 * FORGET_DIM + hd * LANES
        v = proj_ref[:, off:off + LANES].astype(BF16)
        outs = [_dot(a, v[i * seg:(i + 1) * seg]) for i, a in enumerate(seg_scores)]
        for base, m, a in block_scores:
            upd = _dot(a, v[base:base + m])
            for i in range(m // seg):
                outs[(base + m) // seg + i] += upd[i * seg:(i + 1) * seg]
        o = jnp.concatenate(outs, axis=0) + o_inter
        state_ref[hd] = st_decayed + _dot_tn(v, kd)
        return o * lax.rsqrt(jnp.mean(o * o, axis=-1, keepdims=True) + NORM_EPS) * gnw_ref[...]

    def mix_and_output(bounded):
        z0 = 2 * FORGET_DIM + REC_WIDTH
        z_chunks = [(z0 + i * REC_WIDTH // 4, z0 + (i + 1) * REC_WIDTH // 4) for i in range(4)]
        heads_per_chunk = REC_HEADS // len(z_chunks)
        pending = []

        def gate(upto_chunk):
            while pending and pending[0][0] // heads_per_chunk < upto_chunk:
                hd, o = pending.pop(0)
                z = proj_ref[:, z0 + hd * LANES:z0 + (hd + 1) * LANES]
                gated_ref[:, hd * LANES:(hd + 1) * LANES] = (o * (z * _sigmoid(z))).astype(BF16)

        n_gate_chunks = len(z_chunks)
        ahead = 2
        queue = [head_scores(i, bounded) for i in range(ahead)]
        for hd in range(REC_HEADS):
            scores = queue.pop(0)
            if hd + ahead < REC_HEADS:
                queue.append(head_scores(hd + ahead, bounded))
            pending.append((hd, head_finish(hd, scores)))
            gate(min(hd, n_gate_chunks))
            if z_chunks:
                project(*z_chunks.pop(0))
        gate(n_gate_chunks)
        y = _dot(gated_ref[...], wout_ref[...])
        o_ref[...] = x + _rmsnorm(y, postw_ref[...])

    lax.cond(spread <= SAFE_LOG2_SPREAD, lambda: mix_and_output(True), lambda: mix_and_output(False))


def _level_index(c):
    t = np.arange(c)[:, None]
    s = np.arange(c)[None, :]
    lv = np.floor(np.log2(np.maximum(t ^ s, 1))).astype(np.int32)
    lv = np.where(s < t, lv, np.where(s == t, -1, 99))
    return jnp.asarray(lv, dtype=jnp.int32)


def _rec_layer(x, layer, lb_logits, pre_w, post_w, w_in, gnorm_w, w_out):
    B, T, D = x.shape
    chunk = min(REC_CHUNK, T)
    seg = min(REC_SEG, chunk)
    assert T % chunk == 0 and chunk & (chunk - 1) == 0 and seg >= 2 * SUBLANES and D == D_MODEL
    depth = lb_logits.shape[0]
    ltri = jnp.asarray(np.tril(np.ones((chunk, chunk), np.float32)), dtype=BF16)
    bsum_np = np.zeros((max(SUBLANES, chunk // REC_BASE), chunk), np.float32)
    for i in range(chunk // REC_BASE):
        bsum_np[i, i * REC_BASE + 1:(i + 1) * REC_BASE] = 1.0
    bsum = jnp.asarray(bsum_np, dtype=BF16)
    const = lambda b, t: (0, 0)
    return pl.pallas_call(
        functools.partial(_rec_layer_kernel, layer),
        out_shape=jax.ShapeDtypeStruct((B, T, D), x.dtype),
        grid=(B, T // chunk),
        in_specs=[
            pl.BlockSpec((None, chunk, D), lambda b, t: (b, t, 0)),
            pl.BlockSpec((depth, FORGET_DIM), const),
            pl.BlockSpec((seg, seg), const),
            pl.BlockSpec((chunk, chunk), const),
            pl.BlockSpec(bsum.shape, const),
            pl.BlockSpec((1, D), const),
            pl.BlockSpec((1, D), const),
            pl.BlockSpec((D, REC_IN), const),
            pl.BlockSpec((1, REC_VALUE_DIM), const),
            pl.BlockSpec((REC_WIDTH, D), const),
        ],
        out_specs=pl.BlockSpec((None, chunk, D), lambda b, t: (b, t, 0)),
        scratch_shapes=[
            pltpu.VMEM((REC_HEADS, REC_VALUE_DIM, REC_KEY_DIM), F32),
            pltpu.VMEM((chunk, REC_IN), F32),
            pltpu.VMEM((chunk, FORGET_DIM), BF16),
            pltpu.VMEM((chunk, FORGET_DIM), BF16),
            pltpu.VMEM((chunk, FORGET_DIM), BF16),
            pltpu.VMEM((chunk, FORGET_DIM), BF16),
            pltpu.VMEM((REC_HEADS, chunk, REC_KEY_DIM), F32),
            pltpu.VMEM((chunk, REC_WIDTH), BF16),
        ],
        compiler_params=pltpu.CompilerParams(
            dimension_semantics=("arbitrary", "arbitrary"),
            vmem_limit_bytes=VMEM_LIMIT_BYTES),
        name="hgrn2_layer",
    )(x, lb_logits.astype(F32), _level_index(seg), ltri, bsum, pre_w.reshape(1, D), post_w.reshape(1, D),
      w_in.astype(BF16), gnorm_w.reshape(1, REC_VALUE_DIM), w_out.astype(BF16))


def kernel(x, positions, pre_norm_w, post_norm_w, attn_w_in, attn_b_in, attn_sinks, attn_w_out, attn_b_out,
           rec_w_in, rec_lb_logits, rec_gnorm_w, rec_w_out):
    depth = pre_norm_w.shape[0]
    for layer in range(depth):
        j = layer // N_MIXERS
        if layer % N_MIXERS == 0:
            x = _attn_layer(x, positions, pre_norm_w[layer], post_norm_w[layer], attn_w_in[j], attn_b_in[j],
                            attn_sinks[j], attn_w_out[j], attn_b_out[j])
        else:
            x = _rec_layer(x, layer, rec_lb_logits, pre_norm_w[layer], post_norm_w[layer], rec_w_in[j],
                           rec_gnorm_w[j], rec_w_out[j])
    return x
```

```python
import functools

import numpy as np
import jax
import jax.numpy as jnp
from jax import lax
from jax.experimental import pallas as pl
from jax.experimental.pallas import tpu as pltpu

D_MODEL = 1024
N_MIXERS = 2

HEAD_DIM = 64
N_HEADS = D_MODEL // HEAD_DIM
N_KV_HEADS = 2
GROUP = N_HEADS // N_KV_HEADS
ATTN_WIDTH = N_HEADS * HEAD_DIM
KV_WIDTH = N_KV_HEADS * HEAD_DIM
ATTN_IN = 2 * ATTN_WIDTH + 2 * KV_WIDTH
ATTN_BLOCK = 128
ROPE_THETA = 500000.0
ROPE_DIM = HEAD_DIM // 4
ROPE_HALF = ROPE_DIM // 2

REC_HEADS = 8
REC_KEY_DIM = 128
REC_VALUE_DIM = D_MODEL // REC_HEADS
FORGET_DIM = REC_HEADS * REC_KEY_DIM
REC_WIDTH = REC_HEADS * REC_VALUE_DIM
REC_IN = 2 * FORGET_DIM + 2 * REC_WIDTH

NORM_EPS = 1e-6
LOG2_E = 1.4426950408889634

LANES = 128
SUBLANES = 8
ATTN_TILE = 256
REC_CHUNK = 256
REC_SEG = 64
REC_BASE = 64
SAFE_LOG2_EXPONENT = 96.0
VMEM_LIMIT_BYTES = 48 * 1024 * 1024

F32 = jnp.float32
BF16 = jnp.bfloat16


def _staggered_tiles(batch, tiles_per_seq):
    n = batch * tiles_per_seq

    def cur(s):
        c = jnp.minimum(s, n - 1)
        return c // tiles_per_seq, c % tiles_per_seq

    def prev(s):
        p = jnp.maximum(s - 1, 0)
        return p // tiles_per_seq, p % tiles_per_seq

    return n, cur, prev


def _rmsnorm(x, w):
    return x * lax.rsqrt(jnp.mean(x * x, axis=-1, keepdims=True) + NORM_EPS) * w


def _sigmoid(x):
    return 1.0 / (1.0 + jnp.exp(-x))


def _dot(a, b):
    return jnp.dot(a, b, preferred_element_type=F32)


def _dot_nt(a, b):
    return lax.dot_general(a, b, (((1,), (1,)), ((), ())), preferred_element_type=F32)


def _dot_tn(a, b):
    return lax.dot_general(a, b, (((0,), (0,)), ((), ())), preferred_element_type=F32)


def _split_dot_tn(x, p):
    acc = None
    for _ in range(3):
        part = x.astype(BF16)
        x = x - part.astype(F32)
        term = _dot_tn(part, p)
        acc = term if acc is None else acc + term
    return acc


def _attn_layer_kernel(tiles_per_seq, n_tiles, sinks_ref, x_ref, pos_ref, xprev_ref, invf_ref, spread_ref,
                       prew_ref, postw_ref, win_ref, bin_ref, wout_ref, bout_ref, o_ref,
                       kprev_ref, vprev_ref, proj_ref, att_ref, gated_ref):
    s = pl.program_id(0)
    t = jnp.minimum(s, n_tiles - 1) % tiles_per_seq
    tq = x_ref.shape[0]
    n_blocks = tq // ATTN_BLOCK

    @pl.when(s == 0)
    def _():
        gated_ref[...] = jnp.zeros_like(gated_ref)

    @pl.when(t == 0)
    def _():
        kprev_ref[...] = jnp.zeros_like(kprev_ref)
        vprev_ref[...] = jnp.zeros_like(vprev_ref)

    y_prev = _dot(gated_ref[...], wout_ref[...]) + bout_ref[...]

    ang_t = invf_ref[...] * pos_ref[...].astype(F32)
    trig_t = jnp.concatenate([jnp.cos(ang_t), jnp.sin(ang_t)], axis=0)
    tabs = _split_dot_tn(trig_t, spread_ref[...])
    lane = lax.broadcasted_iota(jnp.int32, (1, LANES), 1)
    d = lane % HEAD_DIM
    c_tab = tabs[:, :LANES] + jnp.where(d < ROPE_DIM, 0.0, 1.0)
    s_lo = tabs[:, LANES:2 * LANES]
    s_hi = tabs[:, 2 * LANES:]

    x = x_ref[...]
    h = _rmsnorm(x, prew_ref[...]).astype(BF16)

    def project(c0, c1):
        proj_ref[:, c0:c1] = _dot(h, win_ref[:, c0:c1]) + bin_ref[:, c0:c1]

    q_cols = GROUP * HEAD_DIM
    z0 = ATTN_WIDTH + 2 * KV_WIDTH
    project(ATTN_WIDTH, z0)
    project(0, q_cols)
    project(q_cols, 2 * q_cols)
    later_chunks = [(g * q_cols, (g + 1) * q_cols) for g in range(2, N_KV_HEADS)]
    later_chunks += [(z0 + i * ATTN_WIDTH // 4, z0 + (i + 1) * ATTN_WIDTH // 4) for i in range(4)]

    o_ref[...] = xprev_ref[...] + _rmsnorm(y_prev, postw_ref[...])

    def rope(xc, rows=slice(None)):
        return (xc * c_tab[rows] + pltpu.roll(xc, LANES - ROPE_HALF, 1) * s_lo[rows]
                + pltpu.roll(xc, ROPE_HALF, 1) * s_hi[rows])

    k_all = rope(proj_ref[:, ATTN_WIDTH:ATTN_WIDTH + KV_WIDTH])
    v_all = proj_ref[:, ATTN_WIDTH + KV_WIDTH:ATTN_WIDTH + 2 * KV_WIDTH]

    lo_half = lane < HEAD_DIM
    from_prev = (lax.broadcasted_iota(jnp.int32, (ATTN_BLOCK, ATTN_BLOCK), 1)
                 > lax.broadcasted_iota(jnp.int32, (ATTN_BLOCK, ATTN_BLOCK), 0))
    no_prev = jnp.where(t == 0, -jnp.inf, 0.0)
    scale = HEAD_DIM ** -0.5 * LOG2_E

    def block_diag(a):
        a_sw = pltpu.roll(a, HEAD_DIM, 1)
        zero = jnp.zeros_like(a)
        g0 = jnp.concatenate([jnp.where(lo_half, a, zero), jnp.where(lo_half, zero, a_sw)], axis=0)
        g1 = jnp.concatenate([jnp.where(lo_half, a_sw, zero), jnp.where(lo_half, zero, a)], axis=0)
        return g0.astype(BF16), g1.astype(BF16)

    k_prev = kprev_ref[...]
    v_prev = vprev_ref[...]
    kbs, vbs = [], []
    for j in range(n_blocks):
        rows = slice(j * ATTN_BLOCK, (j + 1) * ATTN_BLOCK)
        k_cur, v_cur = k_all[rows], v_all[rows]
        kbs.append(block_diag(jnp.concatenate([k_prev, k_cur], axis=0)))
        vbs.append(block_diag(jnp.concatenate([v_prev, v_cur], axis=0)))
        k_prev, v_prev = k_cur, v_cur
    kprev_ref[...] = k_prev
    vprev_ref[...] = v_prev

    pairs_per_group = GROUP // 2

    def scores(j, g):
        rows = slice(j * ATTN_BLOCK, (j + 1) * ATTN_BLOCK)
        qs = [(rope(proj_ref[rows, p * LANES:(p + 1) * LANES], rows) * scale).astype(BF16)
              for p in range(g * pairs_per_group, (g + 1) * pairs_per_group)]
        return _dot_nt(jnp.concatenate(qs, axis=0), kbs[j][g])

    items = [(j, g) for g in range(N_KV_HEADS) for j in range(n_blocks)]
    s_next = scores(*items[0])
    for i, (j, g) in enumerate(items):
        s_group = s_next
        if later_chunks:
            project(*later_chunks.pop(0))
        if i + 1 < len(items):
            s_next = scores(*items[i + 1])
        probs, invs = [], []
        for pi in range(pairs_per_group):
            p = g * pairs_per_group + pi
            s = s_group[pi * ATTN_BLOCK:(pi + 1) * ATTN_BLOCK]
            halves, inv = [], []
            for e in range(2):
                s_prev = s[:, 2 * e * ATTN_BLOCK:(2 * e + 1) * ATTN_BLOCK]
                s_cur = s[:, (2 * e + 1) * ATTN_BLOCK:(2 * e + 2) * ATTN_BLOCK]
                if j == 0:
                    s_prev = s_prev + no_prev
                sw = jnp.where(from_prev, s_prev, s_cur)
                sink = sinks_ref[2 * p + e] * LOG2_E
                m = jnp.maximum(jnp.max(sw, axis=-1, keepdims=True), sink)
                pw = jnp.exp2(sw - m)
                denom = jnp.sum(pw, axis=-1, keepdims=True) + jnp.exp2(sink - m)
                pw = pw.astype(BF16)
                zero = jnp.zeros_like(pw)
                halves += [jnp.where(from_prev, pw, zero), jnp.where(from_prev, zero, pw)]
                inv.append(1.0 / denom)
            probs.append(jnp.concatenate(halves, axis=1))
            invs.append(jnp.where(lo_half, inv[0], inv[1]))
        o = _dot(jnp.concatenate(probs, axis=0), vbs[j][g])
        for pi in range(pairs_per_group):
            p = g * pairs_per_group + pi
            att_ref[j * ATTN_BLOCK:(j + 1) * ATTN_BLOCK, p * LANES:(p + 1) * LANES] = (
                o[pi * ATTN_BLOCK:(pi + 1) * ATTN_BLOCK] * invs[pi])
    for chunk in later_chunks:
        project(*chunk)

    z = proj_ref[:, z0:]
    gated_ref[...] = (att_ref[...] * (z * _sigmoid(z))).astype(BF16)


def _rope_spread_matrix():
    p = np.zeros((2 * ROPE_HALF, 3 * LANES), np.float32)
    for lane in range(LANES):
        d = lane % HEAD_DIM
        if d < ROPE_DIM:
            p[d % ROPE_HALF, lane] = 1.0
        if d < ROPE_HALF:
            p[ROPE_HALF + d, LANES + lane] = -1.0
        elif d < ROPE_DIM:
            p[ROPE_HALF + d - ROPE_HALF, 2 * LANES + lane] = 1.0
    return jnp.asarray(p, dtype=BF16)


def _attn_layer(x, positions, pre_w, post_w, w_in, b_in, sinks, w_out, b_out):
    B, T, D = x.shape
    tq = min(ATTN_TILE, T)
    assert T % tq == 0 and tq % ATTN_BLOCK == 0 and D == D_MODEL
    inv_freq = ROPE_THETA ** (-(jnp.arange(ROPE_HALF, dtype=F32) * 2.0 / ROPE_DIM))
    const = lambda s: (0, 0)
    nt = T // tq
    n_tiles, cur, prev = _staggered_tiles(B, nt)
    return pl.pallas_call(
        functools.partial(_attn_layer_kernel, nt, n_tiles),
        out_shape=jax.ShapeDtypeStruct((B, T, D), x.dtype),
        grid=(n_tiles + 1,),
        in_specs=[
            pl.BlockSpec(memory_space=pltpu.SMEM),
            pl.BlockSpec((None, tq, D), lambda s: (*cur(s), 0)),
            pl.BlockSpec((None, 1, tq), lambda s: (cur(s)[0], 0, cur(s)[1])),
            pl.BlockSpec((None, tq, D), lambda s: (*prev(s), 0)),
            pl.BlockSpec((ROPE_HALF, 1), const),
            pl.BlockSpec((2 * ROPE_HALF, 3 * LANES), const),
            pl.BlockSpec((1, D), const),
            pl.BlockSpec((1, D), const),
            pl.BlockSpec((D, ATTN_IN), const),
            pl.BlockSpec((1, ATTN_IN), const),
            pl.BlockSpec((ATTN_WIDTH, D), const),
            pl.BlockSpec((1, D), const),
        ],
        out_specs=pl.BlockSpec((None, tq, D), lambda s: (*prev(s), 0)),
        scratch_shapes=[
            pltpu.VMEM((ATTN_BLOCK, KV_WIDTH), F32),
            pltpu.VMEM((ATTN_BLOCK, KV_WIDTH), F32),
            pltpu.VMEM((tq, ATTN_IN), F32),
            pltpu.VMEM((tq, ATTN_WIDTH), F32),
            pltpu.VMEM((tq, ATTN_WIDTH), BF16),
        ],
        compiler_params=pltpu.CompilerParams(
            dimension_semantics=("arbitrary",),
            vmem_limit_bytes=VMEM_LIMIT_BYTES),
        name="swa_layer",
    )(sinks.astype(F32), x, positions.reshape(B, 1, T), x, inv_freq.reshape(ROPE_HALF, 1), _rope_spread_matrix(),
      pre_w.reshape(1, D), post_w.reshape(1, D),
      w_in.astype(BF16), b_in.reshape(1, ATTN_IN), w_out.astype(BF16), b_out.reshape(1, D))


def _boundary_rows(b, m):
    c, n = b.shape
    if 2 * m >= SUBLANES:
        b3 = b.reshape(c // (2 * m), 2 * m, n)
        return jnp.broadcast_to(b3[:, m - 1:m, :], b3.shape).reshape(c, n)
    b3 = b.reshape(c // SUBLANES, SUBLANES, n)
    sub = lax.broadcasted_iota(jnp.int32, b3.shape, 1)
    r = jnp.broadcast_to(b3[:, SUBLANES - m - 1:SUBLANES - m, :], b3.shape)
    for start in range(SUBLANES - 4 * m, -1, -2 * m):
        r = jnp.where(sub < start + 2 * m, jnp.broadcast_to(b3[:, start + m - 1:start + m, :], b3.shape), r)
    return r.reshape(c, n)


def _rec_layer_kernel(layer, x_ref, lbl_ref, lv_ref, ltri_ref, bsum_ref, prew_ref, postw_ref, win_ref, gnw_ref,
                      wout_ref, o_ref, state_ref, proj_ref, q_ref, k_ref, ghi_ref, glo_ref, b_ref,
                      gated_ref):
    c = pl.program_id(1)
    chunk = x_ref.shape[0]
    seg = lv_ref.shape[0]
    seg_levels = seg.bit_length() - 1
    base_levels = REC_BASE.bit_length() - 1

    @pl.when(c == 0)
    def _():
        state_ref[...] = jnp.zeros_like(state_ref)

    x = x_ref[...]
    h = _rmsnorm(x, prew_ref[...]).astype(BF16)

    def project(c0, c1):
        proj_ref[:, c0:c1] = _dot(h, win_ref[:, c0:c1])

    project(FORGET_DIM, 2 * FORGET_DIM)
    project(0, FORGET_DIM)
    project(2 * FORGET_DIM, 2 * FORGET_DIM + REC_WIDTH)

    def forget_gates(hd):
        off = hd * LANES
        f_raw = proj_ref[:, FORGET_DIM + off:FORGET_DIM + off + LANES]

        logits = lbl_ref[:, off:off + LANES]
        ex = jnp.exp(logits - jnp.max(logits, axis=0, keepdims=True))
        lb = jnp.sum(ex[1:layer + 1], axis=0, keepdims=True) / jnp.sum(ex, axis=0, keepdims=True)

        e = jnp.exp(-jnp.abs(f_raw))
        sp = 1.0 / (1.0 + e)
        pos = f_raw >= 0
        sig = jnp.where(pos, sp, e * sp)
        nsig = jnp.where(pos, e * sp, sp)
        log2_f = jnp.log2(lb + (1.0 - lb) * sig)
        k_ref[:, off:off + LANES] = ((1.0 - lb) * nsig).astype(BF16)
        g_hi = log2_f.astype(BF16)
        ghi_ref[:, off:off + LANES] = g_hi
        glo_ref[:, off:off + LANES] = (log2_f - g_hi.astype(F32)).astype(BF16)

    for hd in range(REC_HEADS):
        forget_gates(hd)

    bsum = bsum_ref[...]
    spread = -jnp.min(_dot(bsum, ghi_ref[...]) + _dot(bsum, glo_ref[...]))

    ltri = ltri_ref[...]
    b_all = _dot(ltri, ghi_ref[...]) + _dot(ltri, glo_ref[...])
    for hd in range(REC_HEADS):
        b_ref[hd] = b_all[:, hd * LANES:(hd + 1) * LANES]
        q_raw = proj_ref[:, hd * LANES:(hd + 1) * LANES]
        q_ref[:, hd * LANES:(hd + 1) * LANES] = (q_raw * _sigmoid(q_raw)).astype(BF16)

    def head_scores(hd, bounded):
        off = hd * LANES
        b = b_ref[hd]
        q = q_ref[:, off:off + LANES]
        k = k_ref[:, off:off + LANES]
        lv = lv_ref[...]

        seg_scores = []
        for s0 in range(0, chunk, seg):
            rows = slice(s0, s0 + seg)
            qs, ks, bs = q[rows], k[rows], b[rows]
            if bounded:
                b3 = bs.reshape(seg // REC_BASE, REC_BASE, LANES)
                mid = 0.5 * (b3[:, 0:1, :] + b3[:, REC_BASE - 1:REC_BASE, :])
                r = jnp.broadcast_to(mid, b3.shape).reshape(seg, LANES)
                a = jnp.where(lv < base_levels,
                              _dot_nt(qs * jnp.exp2(bs - r).astype(BF16), ks * jnp.exp2(r - bs).astype(BF16)), 0.0)
                first_level = base_levels
            else:
                a = jnp.where(lv == -1, _dot_nt(qs, ks), 0.0)
                first_level = 0
            for j in range(first_level, seg_levels):
                en = jnp.exp2(-jnp.abs(bs - _boundary_rows(bs, 1 << j))).astype(BF16)
                a = jnp.where(lv == j, _dot_nt(qs * en, ks * en), a)
            seg_scores.append(a.astype(BF16))
        block_scores = []
        m = seg
        while m < chunk:
            for base in range(0, chunk, 2 * m):
                lo, hi = slice(base, base + m), slice(base + m, base + 2 * m)
                r = b[base + m - 1:base + m]
                qh = q[hi] * jnp.exp2(b[hi] - r).astype(BF16)
                kh = k[lo] * jnp.exp2(r - b[lo]).astype(BF16)
                block_scores.append((base, m, _dot_nt(qh, kh).astype(BF16)))
            m *= 2
        b_last = b[chunk - 1:chunk, :]
        st = state_ref[hd]
        o_inter = _dot_nt(q * jnp.exp2(b).astype(BF16), st.astype(BF16))
        kd = k * jnp.exp2(b_last - b).astype(BF16)
        return seg_scores, block_scores, o_inter, kd, st * jnp.exp2(b_last)

    def head_finish(hd, scores):
        seg_scores, block_scores, o_inter, kd, st_decayed = scores
        off = 2 * FORGET_DIM + hd * LANES
        v = proj_ref[:, off:off + LANES].astype(BF16)
        outs = [_dot(a, v[i * seg:(i + 1) * seg]) for i, a in enumerate(seg_scores)]
        for base, m, a in block_scores:
            upd = _dot(a, v[base:base + m])
            for i in range(m // seg):
                outs[(base + m) // seg + i] += upd[i * seg:(i + 1) * seg]
        o = jnp.concatenate(outs, axis=0) + o_inter
        state_ref[hd] = st_decayed + _dot_tn(v, kd)
        return o * lax.rsqrt(jnp.mean(o * o, axis=-1, keepdims=True) + NORM_EPS) * gnw_ref[...]

    def mix_and_output(bounded):
        z0 = 2 * FORGET_DIM + REC_WIDTH
        z_chunks = [(z0 + i * REC_WIDTH // 4, z0 + (i + 1) * REC_WIDTH // 4) for i in range(4)]
        heads_per_chunk = REC_HEADS // len(z_chunks)
        pending = []

        def gate(upto_chunk):
            while pending and pending[0][0] // heads_per_chunk < upto_chunk:
                hd, o = pending.pop(0)
                z = proj_ref[:, z0 + hd * LANES:z0 + (hd + 1) * LANES]
                gated_ref[:, hd * LANES:(hd + 1) * LANES] = (o * (z * _sigmoid(z))).astype(BF16)

        n_gate_chunks = len(z_chunks)
        ahead = 2
        queue = [head_scores(i, bounded) for i in range(ahead)]
        for hd in range(REC_HEADS):
            scores = queue.pop(0)
            if hd + ahead < REC_HEADS:
                queue.append(head_scores(hd + ahead, bounded))
            pending.append((hd, head_finish(hd, scores)))
            gate(min(hd, n_gate_chunks))
            if z_chunks:
                project(*z_chunks.pop(0))
        gate(n_gate_chunks)
        y = _dot(gated_ref[...], wout_ref[...])
        o_ref[...] = x + _rmsnorm(y, postw_ref[...])

    lax.cond(0.5 * spread <= SAFE_LOG2_EXPONENT, lambda: mix_and_output(True), lambda: mix_and_output(False))


def _level_index(c):
    t = np.arange(c)[:, None]
    s = np.arange(c)[None, :]
    lv = np.floor(np.log2(np.maximum(t ^ s, 1))).astype(np.int32)
    lv = np.where(s < t, lv, np.where(s == t, -1, 99))
    return jnp.asarray(lv, dtype=jnp.int32)


def _rec_layer(x, layer, lb_logits, pre_w, post_w, w_in, gnorm_w, w_out):
    B, T, D = x.shape
    chunk = min(REC_CHUNK, T)
    seg = min(REC_SEG, chunk)
    assert T % chunk == 0 and chunk & (chunk - 1) == 0 and seg >= 2 * SUBLANES and D == D_MODEL
    depth = lb_logits.shape[0]
    ltri = jnp.asarray(np.tril(np.ones((chunk, chunk), np.float32)), dtype=BF16)
    bsum_np = np.zeros((max(SUBLANES, chunk // REC_BASE), chunk), np.float32)
    for i in range(chunk // REC_BASE):
        bsum_np[i, i * REC_BASE + 1:(i + 1) * REC_BASE] = 1.0
    bsum = jnp.asarray(bsum_np, dtype=BF16)
    const = lambda b, t: (0, 0)
    return pl.pallas_call(
        functools.partial(_rec_layer_kernel, layer),
        out_shape=jax.ShapeDtypeStruct((B, T, D), x.dtype),
        grid=(B, T // chunk),
        in_specs=[
            pl.BlockSpec((None, chunk, D), lambda b, t: (b, t, 0)),
            pl.BlockSpec((depth, FORGET_DIM), const),
            pl.BlockSpec((seg, seg), const),
            pl.BlockSpec((chunk, chunk), const),
            pl.BlockSpec(bsum.shape, const),
            pl.BlockSpec((1, D), const),
            pl.BlockSpec((1, D), const),
            pl.BlockSpec((D, REC_IN), const),
            pl.BlockSpec((1, REC_VALUE_DIM), const),
            pl.BlockSpec((REC_WIDTH, D), const),
        ],
        out_specs=pl.BlockSpec((None, chunk, D), lambda b, t: (b, t, 0)),
        scratch_shapes=[
            pltpu.VMEM((REC_HEADS, REC_VALUE_DIM, REC_KEY_DIM), F32),
            pltpu.VMEM((chunk, REC_IN), F32),
            pltpu.VMEM((chunk, FORGET_DIM), BF16),
            pltpu.VMEM((chunk, FORGET_DIM), BF16),
            pltpu.VMEM((chunk, FORGET_DIM), BF16),
            pltpu.VMEM((chunk, FORGET_DIM), BF16),
            pltpu.VMEM((REC_HEADS, chunk, REC_KEY_DIM), F32),
            pltpu.VMEM((chunk, REC_WIDTH), BF16),
        ],
        compiler_params=pltpu.CompilerParams(
            dimension_semantics=("arbitrary", "arbitrary"),
            vmem_limit_bytes=VMEM_LIMIT_BYTES),
        name="hgrn2_layer",
    )(x, lb_logits.astype(F32), _level_index(seg), ltri, bsum, pre_w.reshape(1, D), post_w.reshape(1, D),
      w_in.astype(BF16), gnorm_w.reshape(1, REC_VALUE_DIM), w_out.astype(BF16))


def kernel(x, positions, pre_norm_w, post_norm_w, attn_w_in, attn_b_in, attn_sinks, attn_w_out, attn_b_out,
           rec_w_in, rec_lb_logits, rec_gnorm_w, rec_w_out):
    depth = pre_norm_w.shape[0]
    for layer in range(depth):
        j = layer // N_MIXERS
        if layer % N_MIXERS == 0:
            x = _attn_layer(x, positions, pre_norm_w[layer], post_norm_w[layer], attn_w_in[j], attn_b_in[j],
                            attn_sinks[j], attn_w_out[j], attn_b_out[j])
        else:
            x = _rec_layer(x, layer, rec_lb_logits, pre_norm_w[layer], post_norm_w[layer], rec_w_in[j],
                           rec_gnorm_w[j], rec_w_out[j])
    return x
```

```python
import functools

import numpy as np
import jax
import jax.numpy as jnp
from jax import lax
from jax.experimental import pallas as pl
from jax.experimental.pallas import tpu as pltpu

D_MODEL = 1024
N_MIXERS = 2

HEAD_DIM = 64
N_HEADS = D_MODEL // HEAD_DIM
N_KV_HEADS = 2
GROUP = N_HEADS // N_KV_HEADS
ATTN_WIDTH = N_HEADS * HEAD_DIM
KV_WIDTH = N_KV_HEADS * HEAD_DIM
ATTN_IN = 2 * ATTN_WIDTH + 2 * KV_WIDTH
ATTN_BLOCK = 128
ROPE_THETA = 500000.0
ROPE_DIM = HEAD_DIM // 4
ROPE_HALF = ROPE_DIM // 2

REC_HEADS = 8
REC_KEY_DIM = 128
REC_VALUE_DIM = D_MODEL // REC_HEADS
FORGET_DIM = REC_HEADS * REC_KEY_DIM
REC_WIDTH = REC_HEADS * REC_VALUE_DIM
REC_IN = 2 * FORGET_DIM + 2 * REC_WIDTH

NORM_EPS = 1e-6
LOG2_E = 1.4426950408889634

LANES = 128
SUBLANES = 8
ATTN_TILE = 256
REC_CHUNK = 256
REC_SEG = 128
REC_BASE = 128
SAFE_LOG2_EXPONENT = 112.0
VMEM_LIMIT_BYTES = 48 * 1024 * 1024

F32 = jnp.float32
BF16 = jnp.bfloat16


def _staggered_tiles(batch, tiles_per_seq):
    n = batch * tiles_per_seq

    def cur(s):
        c = jnp.minimum(s, n - 1)
        return c // tiles_per_seq, c % tiles_per_seq

    def prev(s):
        p = jnp.maximum(s - 1, 0)
        return p // tiles_per_seq, p % tiles_per_seq

    return n, cur, prev


def _rmsnorm(x, w):
    return x * lax.rsqrt(jnp.mean(x * x, axis=-1, keepdims=True) + NORM_EPS) * w


def _sigmoid(x):
    return 1.0 / (1.0 + jnp.exp(-x))


def _dot(a, b):
    return jnp.dot(a, b, preferred_element_type=F32)


def _dot_nt(a, b):
    return lax.dot_general(a, b, (((1,), (1,)), ((), ())), preferred_element_type=F32)


def _dot_tn(a, b):
    return lax.dot_general(a, b, (((0,), (0,)), ((), ())), preferred_element_type=F32)


def _split_dot_tn(x, p):
    acc = None
    for _ in range(3):
        part = x.astype(BF16)
        x = x - part.astype(F32)
        term = _dot_tn(part, p)
        acc = term if acc is None else acc + term
    return acc


def _attn_layer_kernel(tiles_per_seq, n_tiles, sinks_ref, x_ref, pos_ref, xprev_ref, invf_ref, spread_ref,
                       prew_ref, postw_ref, win_ref, bin_ref, wout_ref, bout_ref, o_ref,
                       kprev_ref, vprev_ref, proj_ref, att_ref, gated_ref):
    s = pl.program_id(0)
    t = jnp.minimum(s, n_tiles - 1) % tiles_per_seq
    tq = x_ref.shape[0]
    n_blocks = tq // ATTN_BLOCK

    @pl.when(s == 0)
    def _():
        gated_ref[...] = jnp.zeros_like(gated_ref)

    @pl.when(t == 0)
    def _():
        kprev_ref[...] = jnp.zeros_like(kprev_ref)
        vprev_ref[...] = jnp.zeros_like(vprev_ref)

    y_prev = _dot(gated_ref[...], wout_ref[...]) + bout_ref[...]

    ang_t = invf_ref[...] * pos_ref[...].astype(F32)
    trig_t = jnp.concatenate([jnp.cos(ang_t), jnp.sin(ang_t)], axis=0)
    tabs = _split_dot_tn(trig_t, spread_ref[...])
    lane = lax.broadcasted_iota(jnp.int32, (1, LANES), 1)
    d = lane % HEAD_DIM
    c_tab = tabs[:, :LANES] + jnp.where(d < ROPE_DIM, 0.0, 1.0)
    s_lo = tabs[:, LANES:2 * LANES]
    s_hi = tabs[:, 2 * LANES:]

    x = x_ref[...]
    h = _rmsnorm(x, prew_ref[...]).astype(BF16)

    def project(c0, c1):
        proj_ref[:, c0:c1] = _dot(h, win_ref[:, c0:c1]) + bin_ref[:, c0:c1]

    q_cols = GROUP * HEAD_DIM
    z0 = ATTN_WIDTH + 2 * KV_WIDTH
    project(ATTN_WIDTH, z0)
    project(0, q_cols)
    project(q_cols, 2 * q_cols)
    later_chunks = [(g * q_cols, (g + 1) * q_cols) for g in range(2, N_KV_HEADS)]
    later_chunks += [(z0 + i * ATTN_WIDTH // 4, z0 + (i + 1) * ATTN_WIDTH // 4) for i in range(4)]

    o_ref[...] = xprev_ref[...] + _rmsnorm(y_prev, postw_ref[...])

    def rope(xc, rows=slice(None)):
        return (xc * c_tab[rows] + pltpu.roll(xc, LANES - ROPE_HALF, 1) * s_lo[rows]
                + pltpu.roll(xc, ROPE_HALF, 1) * s_hi[rows])

    k_all = rope(proj_ref[:, ATTN_WIDTH:ATTN_WIDTH + KV_WIDTH])
    v_all = proj_ref[:, ATTN_WIDTH + KV_WIDTH:ATTN_WIDTH + 2 * KV_WIDTH]

    lo_half = lane < HEAD_DIM
    from_prev = (lax.broadcasted_iota(jnp.int32, (ATTN_BLOCK, ATTN_BLOCK), 1)
                 > lax.broadcasted_iota(jnp.int32, (ATTN_BLOCK, ATTN_BLOCK), 0))
    no_prev = jnp.where(t == 0, -jnp.inf, 0.0)
    scale = HEAD_DIM ** -0.5 * LOG2_E

    def block_diag(a):
        a_sw = pltpu.roll(a, HEAD_DIM, 1)
        zero = jnp.zeros_like(a)
        g0 = jnp.concatenate([jnp.where(lo_half, a, zero), jnp.where(lo_half, zero, a_sw)], axis=0)
        g1 = jnp.concatenate([jnp.where(lo_half, a_sw, zero), jnp.where(lo_half, zero, a)], axis=0)
        return g0.astype(BF16), g1.astype(BF16)

    k_prev = kprev_ref[...]
    v_prev = vprev_ref[...]
    kbs, vbs = [], []
    for j in range(n_blocks):
        rows = slice(j * ATTN_BLOCK, (j + 1) * ATTN_BLOCK)
        k_cur, v_cur = k_all[rows], v_all[rows]
        kbs.append(block_diag(jnp.concatenate([k_prev, k_cur], axis=0)))
        vbs.append(block_diag(jnp.concatenate([v_prev, v_cur], axis=0)))
        k_prev, v_prev = k_cur, v_cur
    kprev_ref[...] = k_prev
    vprev_ref[...] = v_prev

    pairs_per_group = GROUP // 2

    def scores(j, g):
        rows = slice(j * ATTN_BLOCK, (j + 1) * ATTN_BLOCK)
        qs = [(rope(proj_ref[rows, p * LANES:(p + 1) * LANES], rows) * scale).astype(BF16)
              for p in range(g * pairs_per_group, (g + 1) * pairs_per_group)]
        return _dot_nt(jnp.concatenate(qs, axis=0), kbs[j][g])

    items = [(j, g) for g in range(N_KV_HEADS) for j in range(n_blocks)]
    s_next = scores(*items[0])
    for i, (j, g) in enumerate(items):
        s_group = s_next
        if later_chunks:
            project(*later_chunks.pop(0))
        if i + 1 < len(items):
            s_next = scores(*items[i + 1])
        probs, invs = [], []
        for pi in range(pairs_per_group):
            p = g * pairs_per_group + pi
            s = s_group[pi * ATTN_BLOCK:(pi + 1) * ATTN_BLOCK]
            halves, inv = [], []
            for e in range(2):
                s_prev = s[:, 2 * e * ATTN_BLOCK:(2 * e + 1) * ATTN_BLOCK]
                s_cur = s[:, (2 * e + 1) * ATTN_BLOCK:(2 * e + 2) * ATTN_BLOCK]
                if j == 0:
                    s_prev = s_prev + no_prev
                sw = jnp.where(from_prev, s_prev, s_cur)
                sink = sinks_ref[2 * p + e] * LOG2_E
                m = jnp.maximum(jnp.max(sw, axis=-1, keepdims=True), sink)
                pw = jnp.exp2(sw - m)
                denom = jnp.sum(pw, axis=-1, keepdims=True) + jnp.exp2(sink - m)
                pw = pw.astype(BF16)
                zero = jnp.zeros_like(pw)
                halves += [jnp.where(from_prev, pw, zero), jnp.where(from_prev, zero, pw)]
                inv.append(1.0 / denom)
            probs.append(jnp.concatenate(halves, axis=1))
            invs.append(jnp.where(lo_half, inv[0], inv[1]))
        o = _dot(jnp.concatenate(probs, axis=0), vbs[j][g])
        for pi in range(pairs_per_group):
            p = g * pairs_per_group + pi
            att_ref[j * ATTN_BLOCK:(j + 1) * ATTN_BLOCK, p * LANES:(p + 1) * LANES] = (
                o[pi * ATTN_BLOCK:(pi + 1) * ATTN_BLOCK] * invs[pi])
    for chunk in later_chunks:
        project(*chunk)

    z = proj_ref[:, z0:]
    gated_ref[...] = (att_ref[...] * (z * _sigmoid(z))).astype(BF16)


def _rope_spread_matrix():
    p = np.zeros((2 * ROPE_HALF, 3 * LANES), np.float32)
    for lane in range(LANES):
        d = lane % HEAD_DIM
        if d < ROPE_DIM:
            p[d % ROPE_HALF, lane] = 1.0
        if d < ROPE_HALF:
            p[ROPE_HALF + d, LANES + lane] = -1.0
        elif d < ROPE_DIM:
            p[ROPE_HALF + d - ROPE_HALF, 2 * LANES + lane] = 1.0
    return jnp.asarray(p, dtype=BF16)


def _attn_layer(x, positions, pre_w, post_w, w_in, b_in, sinks, w_out, b_out):
    B, T, D = x.shape
    tq = min(ATTN_TILE, T)
    assert T % tq == 0 and tq % ATTN_BLOCK == 0 and D == D_MODEL
    inv_freq = ROPE_THETA ** (-(jnp.arange(ROPE_HALF, dtype=F32) * 2.0 / ROPE_DIM))
    const = lambda s: (0, 0)
    nt = T // tq
    n_tiles, cur, prev = _staggered_tiles(B, nt)
    return pl.pallas_call(
        functools.partial(_attn_layer_kernel, nt, n_tiles),
        out_shape=jax.ShapeDtypeStruct((B, T, D), x.dtype),
        grid=(n_tiles + 1,),
        in_specs=[
            pl.BlockSpec(memory_space=pltpu.SMEM),
            pl.BlockSpec((None, tq, D), lambda s: (*cur(s), 0)),
            pl.BlockSpec((None, 1, tq), lambda s: (cur(s)[0], 0, cur(s)[1])),
            pl.BlockSpec((None, tq, D), lambda s: (*prev(s), 0)),
            pl.BlockSpec((ROPE_HALF, 1), const),
            pl.BlockSpec((2 * ROPE_HALF, 3 * LANES), const),
            pl.BlockSpec((1, D), const),
            pl.BlockSpec((1, D), const),
            pl.BlockSpec((D, ATTN_IN), const),
            pl.BlockSpec((1, ATTN_IN), const),
            pl.BlockSpec((ATTN_WIDTH, D), const),
            pl.BlockSpec((1, D), const),
        ],
        out_specs=pl.BlockSpec((None, tq, D), lambda s: (*prev(s), 0)),
        scratch_shapes=[
            pltpu.VMEM((ATTN_BLOCK, KV_WIDTH), F32),
            pltpu.VMEM((ATTN_BLOCK, KV_WIDTH), F32),
            pltpu.VMEM((tq, ATTN_IN), F32),
            pltpu.VMEM((tq, ATTN_WIDTH), F32),
            pltpu.VMEM((tq, ATTN_WIDTH), BF16),
        ],
        compiler_params=pltpu.CompilerParams(
            dimension_semantics=("arbitrary",),
            vmem_limit_bytes=VMEM_LIMIT_BYTES),
        name="swa_layer",
    )(sinks.astype(F32), x, positions.reshape(B, 1, T), x, inv_freq.reshape(ROPE_HALF, 1), _rope_spread_matrix(),
      pre_w.reshape(1, D), post_w.reshape(1, D),
      w_in.astype(BF16), b_in.reshape(1, ATTN_IN), w_out.astype(BF16), b_out.reshape(1, D))


def _boundary_rows(b, m):
    c, n = b.shape
    if 2 * m >= SUBLANES:
        b3 = b.reshape(c // (2 * m), 2 * m, n)
        return jnp.broadcast_to(b3[:, m - 1:m, :], b3.shape).reshape(c, n)
    b3 = b.reshape(c // SUBLANES, SUBLANES, n)
    sub = lax.broadcasted_iota(jnp.int32, b3.shape, 1)
    r = jnp.broadcast_to(b3[:, SUBLANES - m - 1:SUBLANES - m, :], b3.shape)
    for start in range(SUBLANES - 4 * m, -1, -2 * m):
        r = jnp.where(sub < start + 2 * m, jnp.broadcast_to(b3[:, start + m - 1:start + m, :], b3.shape), r)
    return r.reshape(c, n)


def _rec_layer_kernel(layer, x_ref, lbl_ref, lv_ref, ltri_ref, bsum_ref, prew_ref, postw_ref, win_ref, gnw_ref,
                      wout_ref, o_ref, state_ref, proj_ref, q_ref, k_ref, ghi_ref, glo_ref, b_ref,
                      gated_ref):
    c = pl.program_id(1)
    chunk = x_ref.shape[0]
    seg = lv_ref.shape[0]
    seg_levels = seg.bit_length() - 1
    base_levels = REC_BASE.bit_length() - 1

    @pl.when(c == 0)
    def _():
        state_ref[...] = jnp.zeros_like(state_ref)

    x = x_ref[...]
    h = _rmsnorm(x, prew_ref[...]).astype(BF16)

    def project(c0, c1):
        proj_ref[:, c0:c1] = _dot(h, win_ref[:, c0:c1])

    project(FORGET_DIM, 2 * FORGET_DIM)
    project(0, FORGET_DIM)
    project(2 * FORGET_DIM, 2 * FORGET_DIM + REC_WIDTH)

    def forget_gates(hd):
        off = hd * LANES
        f_raw = proj_ref[:, FORGET_DIM + off:FORGET_DIM + off + LANES]

        logits = lbl_ref[:, off:off + LANES]
        ex = jnp.exp(logits - jnp.max(logits, axis=0, keepdims=True))
        lb = jnp.sum(ex[1:layer + 1], axis=0, keepdims=True) / jnp.sum(ex, axis=0, keepdims=True)

        e = jnp.exp(-jnp.abs(f_raw))
        sp = 1.0 / (1.0 + e)
        pos = f_raw >= 0
        sig = jnp.where(pos, sp, e * sp)
        nsig = jnp.where(pos, e * sp, sp)
        log2_f = jnp.log2(lb + (1.0 - lb) * sig)
        k_ref[:, off:off + LANES] = ((1.0 - lb) * nsig).astype(BF16)
        g_hi = log2_f.astype(BF16)
        ghi_ref[:, off:off + LANES] = g_hi
        glo_ref[:, off:off + LANES] = (log2_f - g_hi.astype(F32)).astype(BF16)

    for hd in range(REC_HEADS):
        forget_gates(hd)

    bsum = bsum_ref[...]
    spread = -jnp.min(_dot(bsum, ghi_ref[...]) + _dot(bsum, glo_ref[...]))

    ltri = ltri_ref[...]
    b_all = _dot(ltri, ghi_ref[...]) + _dot(ltri, glo_ref[...])
    for hd in range(REC_HEADS):
        b_ref[hd] = b_all[:, hd * LANES:(hd + 1) * LANES]
        q_raw = proj_ref[:, hd * LANES:(hd + 1) * LANES]
        q_ref[:, hd * LANES:(hd + 1) * LANES] = (q_raw * _sigmoid(q_raw)).astype(BF16)

    def head_scores(hd, bounded):
        off = hd * LANES
        b = b_ref[hd]
        q = q_ref[:, off:off + LANES]
        k = k_ref[:, off:off + LANES]
        lv = lv_ref[...]

        seg_scores = []
        for s0 in range(0, chunk, seg):
            rows = slice(s0, s0 + seg)
            qs, ks, bs = q[rows], k[rows], b[rows]
            if bounded:
                b3 = bs.reshape(seg // REC_BASE, REC_BASE, LANES)
                mid = 0.5 * (b3[:, 0:1, :] + b3[:, REC_BASE - 1:REC_BASE, :])
                r = jnp.broadcast_to(mid, b3.shape).reshape(seg, LANES)
                a = jnp.where(lv < base_levels,
                              _dot_nt(qs * jnp.exp2(bs - r).astype(BF16), ks * jnp.exp2(r - bs).astype(BF16)), 0.0)
                first_level = base_levels
            else:
                a = jnp.where(lv == -1, _dot_nt(qs, ks), 0.0)
                first_level = 0
            for j in range(first_level, seg_levels):
                en = jnp.exp2(-jnp.abs(bs - _boundary_rows(bs, 1 << j))).astype(BF16)
                a = jnp.where(lv == j, _dot_nt(qs * en, ks * en), a)
            seg_scores.append(a.astype(BF16))
        block_scores = []
        m = seg
        while m < chunk:
            for base in range(0, chunk, 2 * m):
                lo, hi = slice(base, base + m), slice(base + m, base + 2 * m)
                r = b[base + m - 1:base + m]
                qh = q[hi] * jnp.exp2(b[hi] - r).astype(BF16)
                kh = k[lo] * jnp.exp2(r - b[lo]).astype(BF16)
                block_scores.append((base, m, _dot_nt(qh, kh).astype(BF16)))
            m *= 2
        b_last = b[chunk - 1:chunk, :]
        st = state_ref[hd]
        o_inter = _dot_nt(q * jnp.exp2(b).astype(BF16), st.astype(BF16))
        kd = k * jnp.exp2(b_last - b).astype(BF16)
        return seg_scores, block_scores, o_inter, kd, st * jnp.exp2(b_last)

    def head_finish(hd, scores):
        seg_scores, block_scores, o_inter, kd, st_decayed = scores
        off = 2 * FORGET_DIM + hd * LANES
        v = proj_ref[:, off:off + LANES].astype(BF16)
        outs = [_dot(a, v[i * seg:(i + 1) * seg]) for i, a in enumerate(seg_scores)]
        for base, m, a in block_scores:
            upd = _dot(a, v[base:base + m])
            for i in range(m // seg):
                outs[(base + m) // seg + i] += upd[i * seg:(i + 1) * seg]
        o = jnp.concatenate(outs, axis=0) + o_inter
        state_ref[hd] = st_decayed + _dot_tn(v, kd)
        return o * lax.rsqrt(jnp.mean(o * o, axis=-1, keepdims=True) + NORM_EPS) * gnw_ref[...]

    def mix_and_output(bounded):
        z0 = 2 * FORGET_DIM + REC_WIDTH
        z_chunks = [(z0 + i * REC_WIDTH // 4, z0 + (i + 1) * REC_WIDTH // 4) for i in range(4)]
        heads_per_chunk = REC_HEADS // len(z_chunks)
        pending = []

        def gate(upto_chunk):
            while pending and pending[0][0] // heads_per_chunk < upto_chunk:
                hd, o = pending.pop(0)
                z = proj_ref[:, z0 + hd * LANES:z0 + (hd + 1) * LANES]
                gated_ref[:, hd * LANES:(hd + 1) * LANES] = (o * (z * _sigmoid(z))).astype(BF16)

        n_gate_chunks = len(z_chunks)
        ahead = 2
        queue = [head_scores(i, bounded) for i in range(ahead)]
        for hd in range(REC_HEADS):
            scores = queue.pop(0)
            if hd + ahead < REC_HEADS:
                queue.append(head_scores(hd + ahead, bounded))
            pending.append((hd, head_finish(hd, scores)))
            gate(min(hd, n_gate_chunks))
            if z_chunks:
                project(*z_chunks.pop(0))
        gate(n_gate_chunks)
        y = _dot(gated_ref[...], wout_ref[...])
        o_ref[...] = x + _rmsnorm(y, postw_ref[...])

    lax.cond(0.5 * spread <= SAFE_LOG2_EXPONENT, lambda: mix_and_output(True), lambda: mix_and_output(False))


def _level_index(c):
    t = np.arange(c)[:, None]
    s = np.arange(c)[None, :]
    lv = np.floor(np.log2(np.maximum(t ^ s, 1))).astype(np.int32)
    lv = np.where(s < t, lv, np.where(s == t, -1, 99))
    return jnp.asarray(lv, dtype=jnp.int32)


def _rec_layer(x, layer, lb_logits, pre_w, post_w, w_in, gnorm_w, w_out):
    B, T, D = x.shape
    chunk = min(REC_CHUNK, T)
    seg = min(REC_SEG, chunk)
    assert T % chunk == 0 and chunk & (chunk - 1) == 0 and seg >= 2 * SUBLANES and D == D_MODEL
    depth = lb_logits.shape[0]
    ltri = jnp.asarray(np.tril(np.ones((chunk, chunk), np.float32)), dtype=BF16)
    bsum_np = np.zeros((max(SUBLANES, chunk // REC_BASE), chunk), np.float32)
    for i in range(chunk // REC_BASE):
        bsum_np[i, i * REC_BASE + 1:(i + 1) * REC_BASE] = 1.0
    bsum = jnp.asarray(bsum_np, dtype=BF16)
    const = lambda b, t: (0, 0)
    return pl.pallas_call(
        functools.partial(_rec_layer_kernel, layer),
        out_shape=jax.ShapeDtypeStruct((B, T, D), x.dtype),
        grid=(B, T // chunk),
        in_specs=[
            pl.BlockSpec((None, chunk, D), lambda b, t: (b, t, 0)),
            pl.BlockSpec((depth, FORGET_DIM), const),
            pl.BlockSpec((seg, seg), const),
            pl.BlockSpec((chunk, chunk), const),
            pl.BlockSpec(bsum.shape, const),
            pl.BlockSpec((1, D), const),
            pl.BlockSpec((1, D), const),
            pl.BlockSpec((D, REC_IN), const),
            pl.BlockSpec((1, REC_VALUE_DIM), const),
            pl.BlockSpec((REC_WIDTH, D), const),
        ],
        out_specs=pl.BlockSpec((None, chunk, D), lambda b, t: (b, t, 0)),
        scratch_shapes=[
            pltpu.VMEM((REC_HEADS, REC_VALUE_DIM, REC_KEY_DIM), F32),
            pltpu.VMEM((chunk, REC_IN), F32),
            pltpu.VMEM((chunk, FORGET_DIM), BF16),
            pltpu.VMEM((chunk, FORGET_DIM), BF16),
            pltpu.VMEM((chunk, FORGET_DIM), BF16),
            pltpu.VMEM((chunk, FORGET_DIM), BF16),
            pltpu.VMEM((REC_HEADS, chunk, REC_KEY_DIM), F32),
            pltpu.VMEM((chunk, REC_WIDTH), BF16),
        ],
        compiler_params=pltpu.CompilerParams(
            dimension_semantics=("arbitrary", "arbitrary"),
            vmem_limit_bytes=VMEM_LIMIT_BYTES),
        name="hgrn2_layer",
    )(x, lb_logits.astype(F32), _level_index(seg), ltri, bsum, pre_w.reshape(1, D), post_w.reshape(1, D),
      w_in.astype(BF16), gnorm_w.reshape(1, REC_VALUE_DIM), w_out.astype(BF16))


def kernel(x, positions, pre_norm_w, post_norm_w, attn_w_in, attn_b_in, attn_sinks, attn_w_out, attn_b_out,
           rec_w_in, rec_lb_logits, rec_gnorm_w, rec_w_out):
    depth = pre_norm_w.shape[0]
    for layer in range(depth):
        j = layer // N_MIXERS
        if layer % N_MIXERS == 0:
            x = _attn_layer(x, positions, pre_norm_w[layer], post_norm_w[layer], attn_w_in[j], attn_b_in[j],
                            attn_sinks[j], attn_w_out[j], attn_b_out[j])
        else:
            x = _rec_layer(x, layer, rec_lb_logits, pre_norm_w[layer], post_norm_w[layer], rec_w_in[j],
                           rec_gnorm_w[j], rec_w_out[j])
    return x
```

```python
import functools

import numpy as np
import jax
import jax.numpy as jnp
from jax import lax
from jax.experimental import pallas as pl
from jax.experimental.pallas import tpu as pltpu

D_MODEL = 1024
N_MIXERS = 2

HEAD_DIM = 64
N_HEADS = D_MODEL // HEAD_DIM
N_KV_HEADS = 2
GROUP = N_HEADS // N_KV_HEADS
ATTN_WIDTH = N_HEADS * HEAD_DIM
KV_WIDTH = N_KV_HEADS * HEAD_DIM
ATTN_IN = 2 * ATTN_WIDTH + 2 * KV_WIDTH
ATTN_BLOCK = 128
ROPE_THETA = 500000.0
ROPE_DIM = HEAD_DIM // 4
ROPE_HALF = ROPE_DIM // 2

REC_HEADS = 8
REC_KEY_DIM = 128
REC_VALUE_DIM = D_MODEL // REC_HEADS
FORGET_DIM = REC_HEADS * REC_KEY_DIM
REC_WIDTH = REC_HEADS * REC_VALUE_DIM
REC_IN = 2 * FORGET_DIM + 2 * REC_WIDTH

NORM_EPS = 1e-6
LOG2_E = 1.4426950408889634

LANES = 128
SUBLANES = 8
ATTN_TILE = 256
REC_CHUNK = 256
REC_SEG = 128
REC_BASE = 128
SAFE_LOG2_EXPONENT = 112.0
VMEM_LIMIT_BYTES = 48 * 1024 * 1024

F32 = jnp.float32
BF16 = jnp.bfloat16


def _staggered_tiles(batch, tiles_per_seq):
    n = batch * tiles_per_seq

    def cur(s):
        c = jnp.minimum(s, n - 1)
        return c // tiles_per_seq, c % tiles_per_seq

    def prev(s):
        p = jnp.maximum(s - 1, 0)
        return p // tiles_per_seq, p % tiles_per_seq

    return n, cur, prev


def _rmsnorm(x, w):
    return x * lax.rsqrt(jnp.mean(x * x, axis=-1, keepdims=True) + NORM_EPS) * w


def _sigmoid(x):
    return 1.0 / (1.0 + jnp.exp(-x))


def _dot(a, b):
    return jnp.dot(a, b, preferred_element_type=F32)


def _dot_nt(a, b):
    return lax.dot_general(a, b, (((1,), (1,)), ((), ())), preferred_element_type=F32)


def _dot_tn(a, b):
    return lax.dot_general(a, b, (((0,), (0,)), ((), ())), preferred_element_type=F32)


def _split_dot_tn(x, p):
    acc = None
    for _ in range(3):
        part = x.astype(BF16)
        x = x - part.astype(F32)
        term = _dot_tn(part, p)
        acc = term if acc is None else acc + term
    return acc


def _attn_layer_kernel(tiles_per_seq, n_tiles, sinks_ref, x_ref, pos_ref, xprev_ref, invf_ref, spread_ref,
                       prew_ref, postw_ref, win_ref, bin_ref, wout_ref, bout_ref, o_ref,
                       kprev_ref, vprev_ref, proj_ref, att_ref, gated_ref):
    s = pl.program_id(0)
    t = jnp.minimum(s, n_tiles - 1) % tiles_per_seq
    tq = x_ref.shape[0]
    n_blocks = tq // ATTN_BLOCK

    @pl.when(s == 0)
    def _():
        gated_ref[...] = jnp.zeros_like(gated_ref)

    @pl.when(t == 0)
    def _():
        kprev_ref[...] = jnp.zeros_like(kprev_ref)
        vprev_ref[...] = jnp.zeros_like(vprev_ref)

    y_prev = _dot(gated_ref[...], wout_ref[...]) + bout_ref[...]

    ang_t = invf_ref[...] * pos_ref[...].astype(F32)
    trig_t = jnp.concatenate([jnp.cos(ang_t), jnp.sin(ang_t)], axis=0)
    tabs = _split_dot_tn(trig_t, spread_ref[...])
    lane = lax.broadcasted_iota(jnp.int32, (1, LANES), 1)
    d = lane % HEAD_DIM
    c_tab = tabs[:, :LANES] + jnp.where(d < ROPE_DIM, 0.0, 1.0)
    s_lo = tabs[:, LANES:2 * LANES]
    s_hi = tabs[:, 2 * LANES:]

    x = x_ref[...]
    h = _rmsnorm(x, prew_ref[...]).astype(BF16)

    def project(c0, c1):
        proj_ref[:, c0:c1] = _dot(h, win_ref[:, c0:c1]) + bin_ref[:, c0:c1]

    q_cols = GROUP * HEAD_DIM
    z0 = ATTN_WIDTH + 2 * KV_WIDTH
    project(ATTN_WIDTH, z0)
    project(0, q_cols)
    project(q_cols, 2 * q_cols)
    later_chunks = [(g * q_cols, (g + 1) * q_cols) for g in range(2, N_KV_HEADS)]
    later_chunks += [(z0 + i * ATTN_WIDTH // 4, z0 + (i + 1) * ATTN_WIDTH // 4) for i in range(4)]

    o_ref[...] = xprev_ref[...] + _rmsnorm(y_prev, postw_ref[...])

    def rope(xc, rows=slice(None)):
        return (xc * c_tab[rows] + pltpu.roll(xc, LANES - ROPE_HALF, 1) * s_lo[rows]
                + pltpu.roll(xc, ROPE_HALF, 1) * s_hi[rows])

    k_all = rope(proj_ref[:, ATTN_WIDTH:ATTN_WIDTH + KV_WIDTH])
    v_all = proj_ref[:, ATTN_WIDTH + KV_WIDTH:ATTN_WIDTH + 2 * KV_WIDTH]

    lo_half = lane < HEAD_DIM
    from_prev = (lax.broadcasted_iota(jnp.int32, (ATTN_BLOCK, ATTN_BLOCK), 1)
                 > lax.broadcasted_iota(jnp.int32, (ATTN_BLOCK, ATTN_BLOCK), 0))
    no_prev = jnp.where(t == 0, -jnp.inf, 0.0)
    scale = HEAD_DIM ** -0.5 * LOG2_E

    def block_diag(a):
        a_sw = pltpu.roll(a, HEAD_DIM, 1)
        zero = jnp.zeros_like(a)
        g0 = jnp.concatenate([jnp.where(lo_half, a, zero), jnp.where(lo_half, zero, a_sw)], axis=0)
        g1 = jnp.concatenate([jnp.where(lo_half, a_sw, zero), jnp.where(lo_half, zero, a)], axis=0)
        return g0.astype(BF16), g1.astype(BF16)

    k_prev = kprev_ref[...]
    v_prev = vprev_ref[...]
    kbs, vbs = [], []
    for j in range(n_blocks):
        rows = slice(j * ATTN_BLOCK, (j + 1) * ATTN_BLOCK)
        k_cur, v_cur = k_all[rows], v_all[rows]
        kbs.append(block_diag(jnp.concatenate([k_prev, k_cur], axis=0)))
        vbs.append(block_diag(jnp.concatenate([v_prev, v_cur], axis=0)))
        k_prev, v_prev = k_cur, v_cur
    kprev_ref[...] = k_prev
    vprev_ref[...] = v_prev

    pairs_per_group = GROUP // 2

    def scores(j, g):
        rows = slice(j * ATTN_BLOCK, (j + 1) * ATTN_BLOCK)
        qs = [(rope(proj_ref[rows, p * LANES:(p + 1) * LANES], rows) * scale).astype(BF16)
              for p in range(g * pairs_per_group, (g + 1) * pairs_per_group)]
        return _dot_nt(jnp.concatenate(qs, axis=0), kbs[j][g])

    items = [(j, g) for g in range(N_KV_HEADS) for j in range(n_blocks)]
    s_next = scores(*items[0])
    for i, (j, g) in enumerate(items):
        s_group = s_next
        if later_chunks:
            project(*later_chunks.pop(0))
        if i + 1 < len(items):
            s_next = scores(*items[i + 1])
        probs, invs = [], []
        for pi in range(pairs_per_group):
            p = g * pairs_per_group + pi
            s = s_group[pi * ATTN_BLOCK:(pi + 1) * ATTN_BLOCK]
            halves, inv = [], []
            for e in range(2):
                s_prev = s[:, 2 * e * ATTN_BLOCK:(2 * e + 1) * ATTN_BLOCK]
                s_cur = s[:, (2 * e + 1) * ATTN_BLOCK:(2 * e + 2) * ATTN_BLOCK]
                if j == 0:
                    s_prev = s_prev + no_prev
                sw = jnp.where(from_prev, s_prev, s_cur)
                sink = sinks_ref[2 * p + e] * LOG2_E
                m = jnp.maximum(jnp.max(sw, axis=-1, keepdims=True), sink)
                pw = jnp.exp2(sw - m)
                denom = jnp.sum(pw, axis=-1, keepdims=True) + jnp.exp2(sink - m)
                pw = pw.astype(BF16)
                zero = jnp.zeros_like(pw)
                halves += [jnp.where(from_prev, pw, zero), jnp.where(from_prev, zero, pw)]
                inv.append(1.0 / denom)
            probs.append(jnp.concatenate(halves, axis=1))
            invs.append(jnp.where(lo_half, inv[0], inv[1]))
        o = _dot(jnp.concatenate(probs, axis=0), vbs[j][g])
        for pi in range(pairs_per_group):
            p = g * pairs_per_group + pi
            att_ref[j * ATTN_BLOCK:(j + 1) * ATTN_BLOCK, p * LANES:(p + 1) * LANES] = (
                o[pi * ATTN_BLOCK:(pi + 1) * ATTN_BLOCK] * invs[pi])
    for chunk in later_chunks:
        project(*chunk)

    z = proj_ref[:, z0:]
    gated_ref[...] = (att_ref[...] * (z * _sigmoid(z))).astype(BF16)


def _rope_spread_matrix():
    p = np.zeros((2 * ROPE_HALF, 3 * LANES), np.float32)
    for lane in range(LANES):
        d = lane % HEAD_DIM
        if d < ROPE_DIM:
            p[d % ROPE_HALF, lane] = 1.0
        if d < ROPE_HALF:
            p[ROPE_HALF + d, LANES + lane] = -1.0
        elif d < ROPE_DIM:
            p[ROPE_HALF + d - ROPE_HALF, 2 * LANES + lane] = 1.0
    return jnp.asarray(p, dtype=BF16)


def _attn_layer(x, positions, pre_w, post_w, w_in, b_in, sinks, w_out, b_out):
    B, T, D = x.shape
    tq = min(ATTN_TILE, T)
    assert T % tq == 0 and tq % ATTN_BLOCK == 0 and D == D_MODEL
    inv_freq = ROPE_THETA ** (-(jnp.arange(ROPE_HALF, dtype=F32) * 2.0 / ROPE_DIM))
    const = lambda s: (0, 0)
    nt = T // tq
    n_tiles, cur, prev = _staggered_tiles(B, nt)
    return pl.pallas_call(
        functools.partial(_attn_layer_kernel, nt, n_tiles),
        out_shape=jax.ShapeDtypeStruct((B, T, D), x.dtype),
        grid=(n_tiles + 1,),
        in_specs=[
            pl.BlockSpec(memory_space=pltpu.SMEM),
            pl.BlockSpec((None, tq, D), lambda s: (*cur(s), 0)),
            pl.BlockSpec((None, 1, tq), lambda s: (cur(s)[0], 0, cur(s)[1])),
            pl.BlockSpec((None, tq, D), lambda s: (*prev(s), 0)),
            pl.BlockSpec((ROPE_HALF, 1), const),
            pl.BlockSpec((2 * ROPE_HALF, 3 * LANES), const),
            pl.BlockSpec((1, D), const),
            pl.BlockSpec((1, D), const),
            pl.BlockSpec((D, ATTN_IN), const),
            pl.BlockSpec((1, ATTN_IN), const),
            pl.BlockSpec((ATTN_WIDTH, D), const),
            pl.BlockSpec((1, D), const),
        ],
        out_specs=pl.BlockSpec((None, tq, D), lambda s: (*prev(s), 0)),
        scratch_shapes=[
            pltpu.VMEM((ATTN_BLOCK, KV_WIDTH), F32),
            pltpu.VMEM((ATTN_BLOCK, KV_WIDTH), F32),
            pltpu.VMEM((tq, ATTN_IN), F32),
            pltpu.VMEM((tq, ATTN_WIDTH), F32),
            pltpu.VMEM((tq, ATTN_WIDTH), BF16),
        ],
        compiler_params=pltpu.CompilerParams(
            dimension_semantics=("arbitrary",),
            vmem_limit_bytes=VMEM_LIMIT_BYTES),
        name="swa_layer",
    )(sinks.astype(F32), x, positions.reshape(B, 1, T), x, inv_freq.reshape(ROPE_HALF, 1), _rope_spread_matrix(),
      pre_w.reshape(1, D), post_w.reshape(1, D),
      w_in.astype(BF16), b_in.reshape(1, ATTN_IN), w_out.astype(BF16), b_out.reshape(1, D))


def _boundary_rows(b, m):
    c, n = b.shape
    if 2 * m >= SUBLANES:
        b3 = b.reshape(c // (2 * m), 2 * m, n)
        return jnp.broadcast_to(b3[:, m - 1:m, :], b3.shape).reshape(c, n)
    b3 = b.reshape(c // SUBLANES, SUBLANES, n)
    sub = lax.broadcasted_iota(jnp.int32, b3.shape, 1)
    r = jnp.broadcast_to(b3[:, SUBLANES - m - 1:SUBLANES - m, :], b3.shape)
    for start in range(SUBLANES - 4 * m, -1, -2 * m):
        r = jnp.where(sub < start + 2 * m, jnp.broadcast_to(b3[:, start + m - 1:start + m, :], b3.shape), r)
    return r.reshape(c, n)


def _rec_layer_kernel(layer, chunks_per_seq, n_chunks, x_ref, xprev_ref, lbl_ref, lv_ref, ltri_ref, bsum_ref,
                      prew_ref, postw_ref, win_ref, gnw_ref, wout_ref, o_ref, state_ref, proj_ref, q_ref, k_ref,
                      ghi_ref, glo_ref, b_ref, gated_ref, y_ref):
    s = pl.program_id(0)
    c = jnp.minimum(s, n_chunks - 1) % chunks_per_seq
    chunk = x_ref.shape[0]
    seg = lv_ref.shape[0]
    seg_levels = seg.bit_length() - 1
    base_levels = REC_BASE.bit_length() - 1

    @pl.when(s == 0)
    def _():
        y_ref[...] = jnp.zeros_like(y_ref)

    @pl.when(c == 0)
    def _():
        state_ref[...] = jnp.zeros_like(state_ref)

    x = x_ref[...]
    h = _rmsnorm(x, prew_ref[...]).astype(BF16)

    def project(c0, c1):
        proj_ref[:, c0:c1] = _dot(h, win_ref[:, c0:c1])

    project(FORGET_DIM, 2 * FORGET_DIM)
    project(0, FORGET_DIM)
    project(2 * FORGET_DIM, 2 * FORGET_DIM + REC_WIDTH)

    def forget_gates(hd):
        off = hd * LANES
        f_raw = proj_ref[:, FORGET_DIM + off:FORGET_DIM + off + LANES]

        logits = lbl_ref[:, off:off + LANES]
        ex = jnp.exp(logits - jnp.max(logits, axis=0, keepdims=True))
        lb = jnp.sum(ex[1:layer + 1], axis=0, keepdims=True) / jnp.sum(ex, axis=0, keepdims=True)

        e = jnp.exp(-jnp.abs(f_raw))
        sp = 1.0 / (1.0 + e)
        pos = f_raw >= 0
        sig = jnp.where(pos, sp, e * sp)
        nsig = jnp.where(pos, e * sp, sp)
        log2_f = jnp.log2(lb + (1.0 - lb) * sig)
        k_ref[:, off:off + LANES] = ((1.0 - lb) * nsig).astype(BF16)
        g_hi = log2_f.astype(BF16)
        ghi_ref[:, off:off + LANES] = g_hi
        glo_ref[:, off:off + LANES] = (log2_f - g_hi.astype(F32)).astype(BF16)

    for hd in range(REC_HEADS):
        forget_gates(hd)

    o_ref[...] = xprev_ref[...] + _rmsnorm(y_ref[...], postw_ref[...])

    bsum = bsum_ref[...]
    spread = -jnp.min(_dot(bsum, ghi_ref[...]) + _dot(bsum, glo_ref[...]))

    ltri = ltri_ref[...]
    b_all = _dot(ltri, ghi_ref[...]) + _dot(ltri, glo_ref[...])
    for hd in range(REC_HEADS):
        b_ref[hd] = b_all[:, hd * LANES:(hd + 1) * LANES]
        q_raw = proj_ref[:, hd * LANES:(hd + 1) * LANES]
        q_ref[:, hd * LANES:(hd + 1) * LANES] = (q_raw * _sigmoid(q_raw)).astype(BF16)

    def head_scores(hd, bounded):
        off = hd * LANES
        b = b_ref[hd]
        q = q_ref[:, off:off + LANES]
        k = k_ref[:, off:off + LANES]
        lv = lv_ref[...]

        seg_scores = []
        for s0 in range(0, chunk, seg):
            rows = slice(s0, s0 + seg)
            qs, ks, bs = q[rows], k[rows], b[rows]
            if bounded:
                b3 = bs.reshape(seg // REC_BASE, REC_BASE, LANES)
                mid = 0.5 * (b3[:, 0:1, :] + b3[:, REC_BASE - 1:REC_BASE, :])
                r = jnp.broadcast_to(mid, b3.shape).reshape(seg, LANES)
                a = jnp.where(lv < base_levels,
                              _dot_nt(qs * jnp.exp2(bs - r).astype(BF16), ks * jnp.exp2(r - bs).astype(BF16)), 0.0)
                first_level = base_levels
            else:
                a = jnp.where(lv == -1, _dot_nt(qs, ks), 0.0)
                first_level = 0
            for j in range(first_level, seg_levels):
                en = jnp.exp2(-jnp.abs(bs - _boundary_rows(bs, 1 << j))).astype(BF16)
                a = jnp.where(lv == j, _dot_nt(qs * en, ks * en), a)
            seg_scores.append(a.astype(BF16))
        block_scores = []
        m = seg
        while m < chunk:
            for base in range(0, chunk, 2 * m):
                lo, hi = slice(base, base + m), slice(base + m, base + 2 * m)
                r = b[base + m - 1:base + m]
                qh = q[hi] * jnp.exp2(b[hi] - r).astype(BF16)
                kh = k[lo] * jnp.exp2(r - b[lo]).astype(BF16)
                block_scores.append((base, m, _dot_nt(qh, kh).astype(BF16)))
            m *= 2
        b_last = b[chunk - 1:chunk, :]
        st = state_ref[hd]
        o_inter = _dot_nt(q * jnp.exp2(b).astype(BF16), st.astype(BF16))
        kd = k * jnp.exp2(b_last - b).astype(BF16)
        return seg_scores, block_scores, o_inter, kd, st * jnp.exp2(b_last)

    def head_finish(hd, scores):
        seg_scores, block_scores, o_inter, kd, st_decayed = scores
        off = 2 * FORGET_DIM + hd * LANES
        v = proj_ref[:, off:off + LANES].astype(BF16)
        outs = [_dot(a, v[i * seg:(i + 1) * seg]) for i, a in enumerate(seg_scores)]
        for base, m, a in block_scores:
            upd = _dot(a, v[base:base + m])
            for i in range(m // seg):
                outs[(base + m) // seg + i] += upd[i * seg:(i + 1) * seg]
        o = jnp.concatenate(outs, axis=0) + o_inter
        state_ref[hd] = st_decayed + _dot_tn(v, kd)
        return o * lax.rsqrt(jnp.mean(o * o, axis=-1, keepdims=True) + NORM_EPS) * gnw_ref[...]

    def mix_and_output(bounded):
        z0 = 2 * FORGET_DIM + REC_WIDTH
        z_chunks = [(z0 + i * REC_WIDTH // 4, z0 + (i + 1) * REC_WIDTH // 4) for i in range(4)]
        heads_per_chunk = REC_HEADS // len(z_chunks)
        pending = []

        def gate(upto_chunk):
            while pending and pending[0][0] // heads_per_chunk < upto_chunk:
                hd, o = pending.pop(0)
                z = proj_ref[:, z0 + hd * LANES:z0 + (hd + 1) * LANES]
                gated_ref[:, hd * LANES:(hd + 1) * LANES] = (o * (z * _sigmoid(z))).astype(BF16)

        n_gate_chunks = len(z_chunks)
        ahead = 2
        queue = [head_scores(i, bounded) for i in range(ahead)]
        for hd in range(REC_HEADS):
            scores = queue.pop(0)
            if hd + ahead < REC_HEADS:
                queue.append(head_scores(hd + ahead, bounded))
            pending.append((hd, head_finish(hd, scores)))
            gate(min(hd, n_gate_chunks))
            if z_chunks:
                project(*z_chunks.pop(0))
        gate(n_gate_chunks)
        y_ref[...] = _dot(gated_ref[...], wout_ref[...])

    lax.cond(0.5 * spread <= SAFE_LOG2_EXPONENT, lambda: mix_and_output(True), lambda: mix_and_output(False))


def _level_index(c):
    t = np.arange(c)[:, None]
    s = np.arange(c)[None, :]
    lv = np.floor(np.log2(np.maximum(t ^ s, 1))).astype(np.int32)
    lv = np.where(s < t, lv, np.where(s == t, -1, 99))
    return jnp.asarray(lv, dtype=jnp.int32)


def _rec_layer(x, layer, lb_logits, pre_w, post_w, w_in, gnorm_w, w_out):
    B, T, D = x.shape
    chunk = min(REC_CHUNK, T)
    seg = min(REC_SEG, chunk)
    assert T % chunk == 0 and chunk & (chunk - 1) == 0 and seg >= 2 * SUBLANES and D == D_MODEL
    depth = lb_logits.shape[0]
    ltri = jnp.asarray(np.tril(np.ones((chunk, chunk), np.float32)), dtype=BF16)
    bsum_np = np.zeros((max(SUBLANES, chunk // REC_BASE), chunk), np.float32)
    for i in range(chunk // REC_BASE):
        bsum_np[i, i * REC_BASE + 1:(i + 1) * REC_BASE] = 1.0
    bsum = jnp.asarray(bsum_np, dtype=BF16)
    const = lambda s: (0, 0)
    n_chunks, cur, prev = _staggered_tiles(B, T // chunk)
    return pl.pallas_call(
        functools.partial(_rec_layer_kernel, layer, T // chunk, n_chunks),
        out_shape=jax.ShapeDtypeStruct((B, T, D), x.dtype),
        grid=(n_chunks + 1,),
        in_specs=[
            pl.BlockSpec((None, chunk, D), lambda s: (*cur(s), 0)),
            pl.BlockSpec((None, chunk, D), lambda s: (*prev(s), 0)),
            pl.BlockSpec((depth, FORGET_DIM), const),
            pl.BlockSpec((seg, seg), const),
            pl.BlockSpec((chunk, chunk), const),
            pl.BlockSpec(bsum.shape, const),
            pl.BlockSpec((1, D), const),
            pl.BlockSpec((1, D), const),
            pl.BlockSpec((D, REC_IN), const),
            pl.BlockSpec((1, REC_VALUE_DIM), const),
            pl.BlockSpec((REC_WIDTH, D), const),
        ],
        out_specs=pl.BlockSpec((None, chunk, D), lambda s: (*prev(s), 0)),
        scratch_shapes=[
            pltpu.VMEM((REC_HEADS, REC_VALUE_DIM, REC_KEY_DIM), F32),
            pltpu.VMEM((chunk, REC_IN), F32),
            pltpu.VMEM((chunk, FORGET_DIM), BF16),
            pltpu.VMEM((chunk, FORGET_DIM), BF16),
            pltpu.VMEM((chunk, FORGET_DIM), BF16),
            pltpu.VMEM((chunk, FORGET_DIM), BF16),
            pltpu.VMEM((REC_HEADS, chunk, REC_KEY_DIM), F32),
            pltpu.VMEM((chunk, REC_WIDTH), BF16),
            pltpu.VMEM((chunk, D), F32),
        ],
        compiler_params=pltpu.CompilerParams(
            dimension_semantics=("arbitrary",),
            vmem_limit_bytes=VMEM_LIMIT_BYTES),
        name="hgrn2_layer",
    )(x, x, lb_logits.astype(F32), _level_index(seg), ltri, bsum, pre_w.reshape(1, D), post_w.reshape(1, D),
      w_in.astype(BF16), gnorm_w.reshape(1, REC_VALUE_DIM), w_out.astype(BF16))


def kernel(x, positions, pre_norm_w, post_norm_w, attn_w_in, attn_b_in, attn_sinks, attn_w_out, attn_b_out,
           rec_w_in, rec_lb_logits, rec_gnorm_w, rec_w_out):
    depth = pre_norm_w.shape[0]
    for layer in range(depth):
        j = layer // N_MIXERS
        if layer % N_MIXERS == 0:
            x = _attn_layer(x, positions, pre_norm_w[layer], post_norm_w[layer], attn_w_in[j], attn_b_in[j],
                            attn_sinks[j], attn_w_out[j], attn_b_out[j])
        else:
            x = _rec_layer(x, layer, rec_lb_logits, pre_norm_w[layer], post_norm_w[layer], rec_w_in[j],
                           rec_gnorm_w[j], rec_w_out[j])
    return x
```

```python
import functools

import numpy as np
import jax
import jax.numpy as jnp
from jax import lax
from jax.experimental import pallas as pl
from jax.experimental.pallas import tpu as pltpu

D_MODEL = 1024
N_MIXERS = 2

HEAD_DIM = 64
N_HEADS = D_MODEL // HEAD_DIM
N_KV_HEADS = 2
GROUP = N_HEADS // N_KV_HEADS
ATTN_WIDTH = N_HEADS * HEAD_DIM
KV_WIDTH = N_KV_HEADS * HEAD_DIM
ATTN_IN = 2 * ATTN_WIDTH + 2 * KV_WIDTH
ATTN_BLOCK = 128
ROPE_THETA = 500000.0
ROPE_DIM = HEAD_DIM // 4
ROPE_HALF = ROPE_DIM // 2

REC_HEADS = 8
REC_KEY_DIM = 128
REC_VALUE_DIM = D_MODEL // REC_HEADS
FORGET_DIM = REC_HEADS * REC_KEY_DIM
REC_WIDTH = REC_HEADS * REC_VALUE_DIM
REC_IN = 2 * FORGET_DIM + 2 * REC_WIDTH

NORM_EPS = 1e-6
LOG2_E = 1.4426950408889634

LANES = 128
SUBLANES = 8
ATTN_TILE = 512
REC_CHUNK = 256
REC_SEG = 128
REC_BASE = 128
SAFE_LOG2_EXPONENT = 112.0
VMEM_LIMIT_BYTES = 48 * 1024 * 1024

F32 = jnp.float32
BF16 = jnp.bfloat16


def _staggered_tiles(batch, tiles_per_seq):
    n = batch * tiles_per_seq

    def cur(s):
        c = jnp.minimum(s, n - 1)
        return c // tiles_per_seq, c % tiles_per_seq

    def prev(s):
        p = jnp.maximum(s - 1, 0)
        return p // tiles_per_seq, p % tiles_per_seq

    return n, cur, prev


def _rmsnorm(x, w):
    return x * lax.rsqrt(jnp.mean(x * x, axis=-1, keepdims=True) + NORM_EPS) * w


def _sigmoid(x):
    return 1.0 / (1.0 + jnp.exp(-x))


def _dot(a, b):
    return jnp.dot(a, b, preferred_element_type=F32)


def _dot_nt(a, b):
    return lax.dot_general(a, b, (((1,), (1,)), ((), ())), preferred_element_type=F32)


def _dot_tn(a, b):
    return lax.dot_general(a, b, (((0,), (0,)), ((), ())), preferred_element_type=F32)


def _split_dot_tn(x, p):
    acc = None
    for _ in range(3):
        part = x.astype(BF16)
        x = x - part.astype(F32)
        term = _dot_tn(part, p)
        acc = term if acc is None else acc + term
    return acc


def _attn_layer_kernel(tiles_per_seq, n_tiles, sinks_ref, x_ref, pos_ref, xprev_ref, invf_ref, spread_ref,
                       prew_ref, postw_ref, win_ref, bin_ref, wout_ref, bout_ref, o_ref,
                       kprev_ref, vprev_ref, proj_ref, att_ref, gated_ref):
    s = pl.program_id(0)
    t = jnp.minimum(s, n_tiles - 1) % tiles_per_seq
    tq = x_ref.shape[0]
    n_blocks = tq // ATTN_BLOCK

    @pl.when(s == 0)
    def _():
        gated_ref[...] = jnp.zeros_like(gated_ref)

    @pl.when(t == 0)
    def _():
        kprev_ref[...] = jnp.zeros_like(kprev_ref)
        vprev_ref[...] = jnp.zeros_like(vprev_ref)

    y_prev = _dot(gated_ref[...], wout_ref[...]) + bout_ref[...]

    ang_t = invf_ref[...] * pos_ref[...].astype(F32)
    trig_t = jnp.concatenate([jnp.cos(ang_t), jnp.sin(ang_t)], axis=0)
    tabs = _split_dot_tn(trig_t, spread_ref[...])
    lane = lax.broadcasted_iota(jnp.int32, (1, LANES), 1)
    d = lane % HEAD_DIM
    c_tab = tabs[:, :LANES] + jnp.where(d < ROPE_DIM, 0.0, 1.0)
    s_lo = tabs[:, LANES:2 * LANES]
    s_hi = tabs[:, 2 * LANES:]

    x = x_ref[...]
    h = _rmsnorm(x, prew_ref[...]).astype(BF16)

    def project(c0, c1):
        proj_ref[:, c0:c1] = _dot(h, win_ref[:, c0:c1]) + bin_ref[:, c0:c1]

    q_cols = GROUP * HEAD_DIM
    z0 = ATTN_WIDTH + 2 * KV_WIDTH
    project(ATTN_WIDTH, z0)
    project(0, q_cols)
    project(q_cols, 2 * q_cols)
    later_chunks = [(g * q_cols, (g + 1) * q_cols) for g in range(2, N_KV_HEADS)]
    later_chunks += [(z0 + i * ATTN_WIDTH // 4, z0 + (i + 1) * ATTN_WIDTH // 4) for i in range(4)]

    o_ref[...] = xprev_ref[...] + _rmsnorm(y_prev, postw_ref[...])

    def rope(xc, rows=slice(None)):
        return (xc * c_tab[rows] + pltpu.roll(xc, LANES - ROPE_HALF, 1) * s_lo[rows]
                + pltpu.roll(xc, ROPE_HALF, 1) * s_hi[rows])

    k_all = rope(proj_ref[:, ATTN_WIDTH:ATTN_WIDTH + KV_WIDTH])
    v_all = proj_ref[:, ATTN_WIDTH + KV_WIDTH:ATTN_WIDTH + 2 * KV_WIDTH]

    lo_half = lane < HEAD_DIM
    from_prev = (lax.broadcasted_iota(jnp.int32, (ATTN_BLOCK, ATTN_BLOCK), 1)
                 > lax.broadcasted_iota(jnp.int32, (ATTN_BLOCK, ATTN_BLOCK), 0))
    no_prev = jnp.where(t == 0, -jnp.inf, 0.0)
    scale = HEAD_DIM ** -0.5 * LOG2_E

    def block_diag(a):
        a_sw = pltpu.roll(a, HEAD_DIM, 1)
        zero = jnp.zeros_like(a)
        g0 = jnp.concatenate([jnp.where(lo_half, a, zero), jnp.where(lo_half, zero, a_sw)], axis=0)
        g1 = jnp.concatenate([jnp.where(lo_half, a_sw, zero), jnp.where(lo_half, zero, a)], axis=0)
        return g0.astype(BF16), g1.astype(BF16)

    k_prev = kprev_ref[...]
    v_prev = vprev_ref[...]
    kbs, vbs = [], []
    for j in range(n_blocks):
        rows = slice(j * ATTN_BLOCK, (j + 1) * ATTN_BLOCK)
        k_cur, v_cur = k_all[rows], v_all[rows]
        kbs.append(block_diag(jnp.concatenate([k_prev, k_cur], axis=0)))
        vbs.append(block_diag(jnp.concatenate([v_prev, v_cur], axis=0)))
        k_prev, v_prev = k_cur, v_cur
    kprev_ref[...] = k_prev
    vprev_ref[...] = v_prev

    pairs_per_group = GROUP // 2

    def scores(j, g):
        rows = slice(j * ATTN_BLOCK, (j + 1) * ATTN_BLOCK)
        qs = [(rope(proj_ref[rows, p * LANES:(p + 1) * LANES], rows) * scale).astype(BF16)
              for p in range(g * pairs_per_group, (g + 1) * pairs_per_group)]
        return _dot_nt(jnp.concatenate(qs, axis=0), kbs[j][g])

    items = [(j, g) for g in range(N_KV_HEADS) for j in range(n_blocks)]
    s_next = scores(*items[0])
    for i, (j, g) in enumerate(items):
        s_group = s_next
        if later_chunks:
            project(*later_chunks.pop(0))
        if i + 1 < len(items):
            s_next = scores(*items[i + 1])
        probs, invs = [], []
        for pi in range(pairs_per_group):
            p = g * pairs_per_group + pi
            s = s_group[pi * ATTN_BLOCK:(pi + 1) * ATTN_BLOCK]
            halves, inv = [], []
            for e in range(2):
                s_prev = s[:, 2 * e * ATTN_BLOCK:(2 * e + 1) * ATTN_BLOCK]
                s_cur = s[:, (2 * e + 1) * ATTN_BLOCK:(2 * e + 2) * ATTN_BLOCK]
                if j == 0:
                    s_prev = s_prev + no_prev
                sw = jnp.where(from_prev, s_prev, s_cur)
                sink = sinks_ref[2 * p + e] * LOG2_E
                m = jnp.maximum(jnp.max(sw, axis=-1, keepdims=True), sink)
                pw = jnp.exp2(sw - m)
                denom = jnp.sum(pw, axis=-1, keepdims=True) + jnp.exp2(sink - m)
                pw = pw.astype(BF16)
                zero = jnp.zeros_like(pw)
                halves += [jnp.where(from_prev, pw, zero), jnp.where(from_prev, zero, pw)]
                inv.append(1.0 / denom)
            probs.append(jnp.concatenate(halves, axis=1))
            invs.append(jnp.where(lo_half, inv[0], inv[1]))
        o = _dot(jnp.concatenate(probs, axis=0), vbs[j][g])
        for pi in range(pairs_per_group):
            p = g * pairs_per_group + pi
            att_ref[j * ATTN_BLOCK:(j + 1) * ATTN_BLOCK, p * LANES:(p + 1) * LANES] = (
                o[pi * ATTN_BLOCK:(pi + 1) * ATTN_BLOCK] * invs[pi])
    for chunk in later_chunks:
        project(*chunk)

    z = proj_ref[:, z0:]
    gated_ref[...] = (att_ref[...] * (z * _sigmoid(z))).astype(BF16)


def _rope_spread_matrix():
    p = np.zeros((2 * ROPE_HALF, 3 * LANES), np.float32)
    for lane in range(LANES):
        d = lane % HEAD_DIM
        if d < ROPE_DIM:
            p[d % ROPE_HALF, lane] = 1.0
        if d < ROPE_HALF:
            p[ROPE_HALF + d, LANES + lane] = -1.0
        elif d < ROPE_DIM:
            p[ROPE_HALF + d - ROPE_HALF, 2 * LANES + lane] = 1.0
    return jnp.asarray(p, dtype=BF16)


def _attn_layer(x, positions, pre_w, post_w, w_in, b_in, sinks, w_out, b_out):
    B, T, D = x.shape
    tq = min(ATTN_TILE, T)
    assert T % tq == 0 and tq % ATTN_BLOCK == 0 and D == D_MODEL
    inv_freq = ROPE_THETA ** (-(jnp.arange(ROPE_HALF, dtype=F32) * 2.0 / ROPE_DIM))
    const = lambda s: (0, 0)
    nt = T // tq
    n_tiles, cur, prev = _staggered_tiles(B, nt)
    return pl.pallas_call(
        functools.partial(_attn_layer_kernel, nt, n_tiles),
        out_shape=jax.ShapeDtypeStruct((B, T, D), x.dtype),
        grid=(n_tiles + 1,),
        in_specs=[
            pl.BlockSpec(memory_space=pltpu.SMEM),
            pl.BlockSpec((None, tq, D), lambda s: (*cur(s), 0)),
            pl.BlockSpec((None, 1, tq), lambda s: (cur(s)[0], 0, cur(s)[1])),
            pl.BlockSpec((None, tq, D), lambda s: (*prev(s), 0)),
            pl.BlockSpec((ROPE_HALF, 1), const),
            pl.BlockSpec((2 * ROPE_HALF, 3 * LANES), const),
            pl.BlockSpec((1, D), const),
            pl.BlockSpec((1, D), const),
            pl.BlockSpec((D, ATTN_IN), const),
            pl.BlockSpec((1, ATTN_IN), const),
            pl.BlockSpec((ATTN_WIDTH, D), const),
            pl.BlockSpec((1, D), const),
        ],
        out_specs=pl.BlockSpec((None, tq, D), lambda s: (*prev(s), 0)),
        scratch_shapes=[
            pltpu.VMEM((ATTN_BLOCK, KV_WIDTH), F32),
            pltpu.VMEM((ATTN_BLOCK, KV_WIDTH), F32),
            pltpu.VMEM((tq, ATTN_IN), F32),
            pltpu.VMEM((tq, ATTN_WIDTH), F32),
            pltpu.VMEM((tq, ATTN_WIDTH), BF16),
        ],
        compiler_params=pltpu.CompilerParams(
            dimension_semantics=("arbitrary",),
            vmem_limit_bytes=VMEM_LIMIT_BYTES),
        name="swa_layer",
    )(sinks.astype(F32), x, positions.reshape(B, 1, T), x, inv_freq.reshape(ROPE_HALF, 1), _rope_spread_matrix(),
      pre_w.reshape(1, D), post_w.reshape(1, D),
      w_in.astype(BF16), b_in.reshape(1, ATTN_IN), w_out.astype(BF16), b_out.reshape(1, D))


def _boundary_rows(b, m):
    c, n = b.shape
    if 2 * m >= SUBLANES:
        b3 = b.reshape(c // (2 * m), 2 * m, n)
        return jnp.broadcast_to(b3[:, m - 1:m, :], b3.shape).reshape(c, n)
    b3 = b.reshape(c // SUBLANES, SUBLANES, n)
    sub = lax.broadcasted_iota(jnp.int32, b3.shape, 1)
    r = jnp.broadcast_to(b3[:, SUBLANES - m - 1:SUBLANES - m, :], b3.shape)
    for start in range(SUBLANES - 4 * m, -1, -2 * m):
        r = jnp.where(sub < start + 2 * m, jnp.broadcast_to(b3[:, start + m - 1:start + m, :], b3.shape), r)
    return r.reshape(c, n)


def _rec_layer_kernel(layer, chunks_per_seq, n_chunks, x_ref, xprev_ref, lbl_ref, lv_ref, ltri_ref, bsum_ref,
                      prew_ref, postw_ref, win_ref, gnw_ref, wout_ref, o_ref, state_ref, proj_ref, q_ref, k_ref,
                      ghi_ref, glo_ref, b_ref, gated_ref, y_ref):
    s = pl.program_id(0)
    c = jnp.minimum(s, n_chunks - 1) % chunks_per_seq
    chunk = x_ref.shape[0]
    seg = lv_ref.shape[0]
    seg_levels = seg.bit_length() - 1
    base_levels = REC_BASE.bit_length() - 1

    @pl.when(s == 0)
    def _():
        y_ref[...] = jnp.zeros_like(y_ref)

    @pl.when(c == 0)
    def _():
        state_ref[...] = jnp.zeros_like(state_ref)

    x = x_ref[...]
    h = _rmsnorm(x, prew_ref[...]).astype(BF16)

    def project(c0, c1):
        proj_ref[:, c0:c1] = _dot(h, win_ref[:, c0:c1])

    project(FORGET_DIM, 2 * FORGET_DIM)
    project(0, FORGET_DIM)
    project(2 * FORGET_DIM, 2 * FORGET_DIM + REC_WIDTH)

    def forget_gates(hd):
        off = hd * LANES
        f_raw = proj_ref[:, FORGET_DIM + off:FORGET_DIM + off + LANES]

        logits = lbl_ref[:, off:off + LANES]
        ex = jnp.exp(logits - jnp.max(logits, axis=0, keepdims=True))
        lb = jnp.sum(ex[1:layer + 1], axis=0, keepdims=True) / jnp.sum(ex, axis=0, keepdims=True)

        e = jnp.exp(-jnp.abs(f_raw))
        sp = 1.0 / (1.0 + e)
        pos = f_raw >= 0
        sig = jnp.where(pos, sp, e * sp)
        nsig = jnp.where(pos, e * sp, sp)
        log2_f = jnp.log2(lb + (1.0 - lb) * sig)
        k_ref[:, off:off + LANES] = ((1.0 - lb) * nsig).astype(BF16)
        g_hi = log2_f.astype(BF16)
        ghi_ref[:, off:off + LANES] = g_hi
        glo_ref[:, off:off + LANES] = (log2_f - g_hi.astype(F32)).astype(BF16)

    for hd in range(REC_HEADS):
        forget_gates(hd)

    o_ref[...] = xprev_ref[...] + _rmsnorm(y_ref[...], postw_ref[...])

    bsum = bsum_ref[...]
    spread = -jnp.min(_dot(bsum, ghi_ref[...]) + _dot(bsum, glo_ref[...]))

    ltri = ltri_ref[...]
    b_all = _dot(ltri, ghi_ref[...]) + _dot(ltri, glo_ref[...])
    for hd in range(REC_HEADS):
        b_ref[hd] = b_all[:, hd * LANES:(hd + 1) * LANES]
        q_raw = proj_ref[:, hd * LANES:(hd + 1) * LANES]
        q_ref[:, hd * LANES:(hd + 1) * LANES] = (q_raw * _sigmoid(q_raw)).astype(BF16)

    def head_scores(hd, bounded):
        off = hd * LANES
        b = b_ref[hd]
        q = q_ref[:, off:off + LANES]
        k = k_ref[:, off:off + LANES]
        lv = lv_ref[...]

        seg_scores = []
        for s0 in range(0, chunk, seg):
            rows = slice(s0, s0 + seg)
            qs, ks, bs = q[rows], k[rows], b[rows]
            if bounded:
                b3 = bs.reshape(seg // REC_BASE, REC_BASE, LANES)
                mid = 0.5 * (b3[:, 0:1, :] + b3[:, REC_BASE - 1:REC_BASE, :])
                r = jnp.broadcast_to(mid, b3.shape).reshape(seg, LANES)
                a = jnp.where(lv < base_levels,
                              _dot_nt(qs * jnp.exp2(bs - r).astype(BF16), ks * jnp.exp2(r - bs).astype(BF16)), 0.0)
                first_level = base_levels
            else:
                a = jnp.where(lv == -1, _dot_nt(qs, ks), 0.0)
                first_level = 0
            for j in range(first_level, seg_levels):
                en = jnp.exp2(-jnp.abs(bs - _boundary_rows(bs, 1 << j))).astype(BF16)
                a = jnp.where(lv == j, _dot_nt(qs * en, ks * en), a)
            seg_scores.append(a.astype(BF16))
        block_scores = []
        m = seg
        while m < chunk:
            for base in range(0, chunk, 2 * m):
                lo, hi = slice(base, base + m), slice(base + m, base + 2 * m)
                r = b[base + m - 1:base + m]
                qh = q[hi] * jnp.exp2(b[hi] - r).astype(BF16)
                kh = k[lo] * jnp.exp2(r - b[lo]).astype(BF16)
                block_scores.append((base, m, _dot_nt(qh, kh).astype(BF16)))
            m *= 2
        b_last = b[chunk - 1:chunk, :]
        st = state_ref[hd]
        o_inter = _dot_nt(q * jnp.exp2(b).astype(BF16), st.astype(BF16))
        kd = k * jnp.exp2(b_last - b).astype(BF16)
        return seg_scores, block_scores, o_inter, kd, st * jnp.exp2(b_last)

    def head_finish(hd, scores):
        seg_scores, block_scores, o_inter, kd, st_decayed = scores
        off = 2 * FORGET_DIM + hd * LANES
        v = proj_ref[:, off:off + LANES].astype(BF16)
        outs = [_dot(a, v[i * seg:(i + 1) * seg]) for i, a in enumerate(seg_scores)]
        for base, m, a in block_scores:
            upd = _dot(a, v[base:base + m])
            for i in range(m // seg):
                outs[(base + m) // seg + i] += upd[i * seg:(i + 1) * seg]
        o = jnp.concatenate(outs, axis=0) + o_inter
        state_ref[hd] = st_decayed + _dot_tn(v, kd)
        return o * lax.rsqrt(jnp.mean(o * o, axis=-1, keepdims=True) + NORM_EPS) * gnw_ref[...]

    def mix_and_output(bounded):
        z0 = 2 * FORGET_DIM + REC_WIDTH
        z_chunks = [(z0 + i * REC_WIDTH // 4, z0 + (i + 1) * REC_WIDTH // 4) for i in range(4)]
        heads_per_chunk = REC_HEADS // len(z_chunks)
        pending = []

        def gate(upto_chunk):
            while pending and pending[0][0] // heads_per_chunk < upto_chunk:
                hd, o = pending.pop(0)
                z = proj_ref[:, z0 + hd * LANES:z0 + (hd + 1) * LANES]
                gated_ref[:, hd * LANES:(hd + 1) * LANES] = (o * (z * _sigmoid(z))).astype(BF16)

        n_gate_chunks = len(z_chunks)
        ahead = 2
        queue = [head_scores(i, bounded) for i in range(ahead)]
        for hd in range(REC_HEADS):
            scores = queue.pop(0)
            if hd + ahead < REC_HEADS:
                queue.append(head_scores(hd + ahead, bounded))
            pending.append((hd, head_finish(hd, scores)))
            gate(min(hd, n_gate_chunks))
            if z_chunks:
                project(*z_chunks.pop(0))
        gate(n_gate_chunks)
        y_ref[...] = _dot(gated_ref[...], wout_ref[...])

    lax.cond(0.5 * spread <= SAFE_LOG2_EXPONENT, lambda: mix_and_output(True), lambda: mix_and_output(False))


def _level_index(c):
    t = np.arange(c)[:, None]
    s = np.arange(c)[None, :]
    lv = np.floor(np.log2(np.maximum(t ^ s, 1))).astype(np.int32)
    lv = np.where(s < t, lv, np.where(s == t, -1, 99))
    return jnp.asarray(lv, dtype=jnp.int32)


def _rec_layer(x, layer, lb_logits, pre_w, post_w, w_in, gnorm_w, w_out):
    B, T, D = x.shape
    chunk = min(REC_CHUNK, T)
    seg = min(REC_SEG, chunk)
    assert T % chunk == 0 and chunk & (chunk - 1) == 0 and seg >= 2 * SUBLANES and D == D_MODEL
    depth = lb_logits.shape[0]
    ltri = jnp.asarray(np.tril(np.ones((chunk, chunk), np.float32)), dtype=BF16)
    bsum_np = np.zeros((max(SUBLANES, chunk // REC_BASE), chunk), np.float32)
    for i in range(chunk // REC_BASE):
        bsum_np[i, i * REC_BASE + 1:(i + 1) * REC_BASE] = 1.0
    bsum = jnp.asarray(bsum_np, dtype=BF16)
    const = lambda s: (0, 0)
    n_chunks, cur, prev = _staggered_tiles(B, T // chunk)
    return pl.pallas_call(
        functools.partial(_rec_layer_kernel, layer, T // chunk, n_chunks),
        out_shape=jax.ShapeDtypeStruct((B, T, D), x.dtype),
        grid=(n_chunks + 1,),
        in_specs=[
            pl.BlockSpec((None, chunk, D), lambda s: (*cur(s), 0)),
            pl.BlockSpec((None, chunk, D), lambda s: (*prev(s), 0)),
            pl.BlockSpec((depth, FORGET_DIM), const),
            pl.BlockSpec((seg, seg), const),
            pl.BlockSpec((chunk, chunk), const),
            pl.BlockSpec(bsum.shape, const),
            pl.BlockSpec((1, D), const),
            pl.BlockSpec((1, D), const),
            pl.BlockSpec((D, REC_IN), const),
            pl.BlockSpec((1, REC_VALUE_DIM), const),
            pl.BlockSpec((REC_WIDTH, D), const),
        ],
        out_specs=pl.BlockSpec((None, chunk, D), lambda s: (*prev(s), 0)),
        scratch_shapes=[
            pltpu.VMEM((REC_HEADS, REC_VALUE_DIM, REC_KEY_DIM), F32),
            pltpu.VMEM((chunk, REC_IN), F32),
            pltpu.VMEM((chunk, FORGET_DIM), BF16),
            pltpu.VMEM((chunk, FORGET_DIM), BF16),
            pltpu.VMEM((chunk, FORGET_DIM), BF16),
            pltpu.VMEM((chunk, FORGET_DIM), BF16),
            pltpu.VMEM((REC_HEADS, chunk, REC_KEY_DIM), F32),
            pltpu.VMEM((chunk, REC_WIDTH), BF16),
            pltpu.VMEM((chunk, D), F32),
        ],
        compiler_params=pltpu.CompilerParams(
            dimension_semantics=("arbitrary",),
            vmem_limit_bytes=VMEM_LIMIT_BYTES),
        name="hgrn2_layer",
    )(x, x, lb_logits.astype(F32), _level_index(seg), ltri, bsum, pre_w.reshape(1, D), post_w.reshape(1, D),
      w_in.astype(BF16), gnorm_w.reshape(1, REC_VALUE_DIM), w_out.astype(BF16))


def kernel(x, positions, pre_norm_w, post_norm_w, attn_w_in, attn_b_in, attn_sinks, attn_w_out, attn_b_out,
           rec_w_in, rec_lb_logits, rec_gnorm_w, rec_w_out):
    depth = pre_norm_w.shape[0]
    for layer in range(depth):
        j = layer // N_MIXERS
        if layer % N_MIXERS == 0:
            x = _attn_layer(x, positions, pre_norm_w[layer], post_norm_w[layer], attn_w_in[j], attn_b_in[j],
                            attn_sinks[j], attn_w_out[j], attn_b_out[j])
        else:
            x = _rec_layer(x, layer, rec_lb_logits, pre_norm_w[layer], post_norm_w[layer], rec_w_in[j],
                           rec_gnorm_w[j], rec_w_out[j])
    return x
```

```python
import functools

import numpy as np
import jax
import jax.numpy as jnp
from jax import lax
from jax.experimental import pallas as pl
from jax.experimental.pallas import tpu as pltpu

D_MODEL = 1024
N_MIXERS = 2

HEAD_DIM = 64
N_HEADS = D_MODEL // HEAD_DIM
N_KV_HEADS = 2
GROUP = N_HEADS // N_KV_HEADS
ATTN_WIDTH = N_HEADS * HEAD_DIM
KV_WIDTH = N_KV_HEADS * HEAD_DIM
ATTN_IN = 2 * ATTN_WIDTH + 2 * KV_WIDTH
ATTN_BLOCK = 128
ROPE_THETA = 500000.0
ROPE_DIM = HEAD_DIM // 4
ROPE_HALF = ROPE_DIM // 2

REC_HEADS = 8
REC_KEY_DIM = 128
REC_VALUE_DIM = D_MODEL // REC_HEADS
FORGET_DIM = REC_HEADS * REC_KEY_DIM
REC_WIDTH = REC_HEADS * REC_VALUE_DIM
REC_IN = 2 * FORGET_DIM + 2 * REC_WIDTH

NORM_EPS = 1e-6
LOG2_E = 1.4426950408889634

LANES = 128
SUBLANES = 8
ATTN_TILE = 512
REC_CHUNK = 256
REC_SEG = 128
REC_BASE = 128
SAFE_LOG2_EXPONENT = 112.0
VMEM_LIMIT_BYTES = 48 * 1024 * 1024

F32 = jnp.float32
BF16 = jnp.bfloat16


def _staggered_tiles(batch, tiles_per_seq):
    n = batch * tiles_per_seq

    def cur(s):
        c = jnp.minimum(s, n - 1)
        return c // tiles_per_seq, c % tiles_per_seq

    def prev(s):
        p = jnp.maximum(s - 1, 0)
        return p // tiles_per_seq, p % tiles_per_seq

    return n, cur, prev


def _rmsnorm(x, w):
    return x * lax.rsqrt(jnp.mean(x * x, axis=-1, keepdims=True) + NORM_EPS) * w


def _sigmoid(x):
    return 1.0 / (1.0 + jnp.exp(-x))


def _dot(a, b):
    return jnp.dot(a, b, preferred_element_type=F32)


def _dot_nt(a, b):
    return lax.dot_general(a, b, (((1,), (1,)), ((), ())), preferred_element_type=F32)


def _dot_tn(a, b):
    return lax.dot_general(a, b, (((0,), (0,)), ((), ())), preferred_element_type=F32)


def _split_dot_tn(x, p):
    acc = None
    for _ in range(3):
        part = x.astype(BF16)
        x = x - part.astype(F32)
        term = _dot_tn(part, p)
        acc = term if acc is None else acc + term
    return acc


def _attn_layer_kernel(tiles_per_seq, n_tiles, sinks_ref, x_ref, pos_ref, xprev_ref, invf_ref, spread_ref,
                       prew_ref, postw_ref, win_ref, bin_ref, wout_ref, bout_ref, o_ref,
                       kprev_ref, vprev_ref, proj_ref, att_ref, gated_ref):
    s = pl.program_id(0)
    t = jnp.minimum(s, n_tiles - 1) % tiles_per_seq
    tq = x_ref.shape[0]
    n_blocks = tq // ATTN_BLOCK

    @pl.when(s == 0)
    def _():
        gated_ref[...] = jnp.zeros_like(gated_ref)

    @pl.when(t == 0)
    def _():
        kprev_ref[...] = jnp.zeros_like(kprev_ref)
        vprev_ref[...] = jnp.zeros_like(vprev_ref)

    y_prev = _dot(gated_ref[...], wout_ref[...]) + bout_ref[...]

    ang_t = invf_ref[...] * pos_ref[...].astype(F32)
    trig_t = jnp.concatenate([jnp.cos(ang_t), jnp.sin(ang_t)], axis=0)
    tabs = _split_dot_tn(trig_t, spread_ref[...])
    lane = lax.broadcasted_iota(jnp.int32, (1, LANES), 1)
    d = lane % HEAD_DIM
    c_tab = tabs[:, :LANES] + jnp.where(d < ROPE_DIM, 0.0, 1.0)
    s_lo = tabs[:, LANES:2 * LANES]
    s_hi = tabs[:, 2 * LANES:]

    x = x_ref[...]
    h = _rmsnorm(x, prew_ref[...]).astype(BF16)

    def project(c0, c1):
        proj_ref[:, c0:c1] = _dot(h, win_ref[:, c0:c1]) + bin_ref[:, c0:c1]

    q_cols = GROUP * HEAD_DIM
    z0 = ATTN_WIDTH + 2 * KV_WIDTH
    project(ATTN_WIDTH, z0)
    project(0, q_cols)
    project(q_cols, 2 * q_cols)
    later_chunks = [(g * q_cols, (g + 1) * q_cols) for g in range(2, N_KV_HEADS)]
    later_chunks += [(z0 + i * ATTN_WIDTH // 4, z0 + (i + 1) * ATTN_WIDTH // 4) for i in range(4)]

    o_ref[...] = xprev_ref[...] + _rmsnorm(y_prev, postw_ref[...])

    def rope(xc, rows=slice(None)):
        return (xc * c_tab[rows] + pltpu.roll(xc, LANES - ROPE_HALF, 1) * s_lo[rows]
                + pltpu.roll(xc, ROPE_HALF, 1) * s_hi[rows])

    k_all = rope(proj_ref[:, ATTN_WIDTH:ATTN_WIDTH + KV_WIDTH])
    v_all = proj_ref[:, ATTN_WIDTH + KV_WIDTH:ATTN_WIDTH + 2 * KV_WIDTH]

    lo_half = lane < HEAD_DIM
    from_prev = (lax.broadcasted_iota(jnp.int32, (ATTN_BLOCK, ATTN_BLOCK), 1)
                 > lax.broadcasted_iota(jnp.int32, (ATTN_BLOCK, ATTN_BLOCK), 0))
    no_prev = jnp.where(t == 0, -jnp.inf, 0.0)
    scale = HEAD_DIM ** -0.5 * LOG2_E

    def block_diag(a):
        a_sw = pltpu.roll(a, HEAD_DIM, 1)
        zero = jnp.zeros_like(a)
        g0 = jnp.concatenate([jnp.where(lo_half, a, zero), jnp.where(lo_half, zero, a_sw)], axis=0)
        g1 = jnp.concatenate([jnp.where(lo_half, a_sw, zero), jnp.where(lo_half, zero, a)], axis=0)
        return g0.astype(BF16), g1.astype(BF16)

    k_prev = kprev_ref[...]
    v_prev = vprev_ref[...]
    kbs, vbs = [], []
    for j in range(n_blocks):
        rows = slice(j * ATTN_BLOCK, (j + 1) * ATTN_BLOCK)
        k_cur, v_cur = k_all[rows], v_all[rows]
        kbs.append(block_diag(jnp.concatenate([k_prev, k_cur], axis=0)))
        vbs.append(block_diag(jnp.concatenate([v_prev, v_cur], axis=0)))
        k_prev, v_prev = k_cur, v_cur
    kprev_ref[...] = k_prev
    vprev_ref[...] = v_prev

    pairs_per_group = GROUP // 2

    def scores(j, g):
        rows = slice(j * ATTN_BLOCK, (j + 1) * ATTN_BLOCK)
        qs = [(rope(proj_ref[rows, p * LANES:(p + 1) * LANES], rows) * scale).astype(BF16)
              for p in range(g * pairs_per_group, (g + 1) * pairs_per_group)]
        return _dot_nt(jnp.concatenate(qs, axis=0), kbs[j][g])

    items = [(j, g) for g in range(N_KV_HEADS) for j in range(n_blocks)]
    s_next = scores(*items[0])
    for i, (j, g) in enumerate(items):
        s_group = s_next
        if later_chunks:
            project(*later_chunks.pop(0))
        if i + 1 < len(items):
            s_next = scores(*items[i + 1])
        probs, invs = [], []
        for pi in range(pairs_per_group):
            p = g * pairs_per_group + pi
            s = s_group[pi * ATTN_BLOCK:(pi + 1) * ATTN_BLOCK]
            halves, inv = [], []
            for e in range(2):
                s_prev = s[:, 2 * e * ATTN_BLOCK:(2 * e + 1) * ATTN_BLOCK]
                s_cur = s[:, (2 * e + 1) * ATTN_BLOCK:(2 * e + 2) * ATTN_BLOCK]
                if j == 0:
                    s_prev = s_prev + no_prev
                sw = jnp.where(from_prev, s_prev, s_cur)
                sink = sinks_ref[2 * p + e] * LOG2_E
                m = jnp.maximum(jnp.max(sw, axis=-1, keepdims=True), sink)
                pw = jnp.exp2(sw - m)
                denom = jnp.sum(pw, axis=-1, keepdims=True) + jnp.exp2(sink - m)
                pw = pw.astype(BF16)
                zero = jnp.zeros_like(pw)
                halves += [jnp.where(from_prev, pw, zero), jnp.where(from_prev, zero, pw)]
                inv.append(1.0 / denom)
            probs.append(jnp.concatenate(halves, axis=1))
            invs.append(jnp.where(lo_half, inv[0], inv[1]))
        o = _dot(jnp.concatenate(probs, axis=0), vbs[j][g])
        for pi in range(pairs_per_group):
            p = g * pairs_per_group + pi
            att_ref[j * ATTN_BLOCK:(j + 1) * ATTN_BLOCK, p * LANES:(p + 1) * LANES] = (
                o[pi * ATTN_BLOCK:(pi + 1) * ATTN_BLOCK] * invs[pi])
    for chunk in later_chunks:
        project(*chunk)

    z = proj_ref[:, z0:]
    gated_ref[...] = (att_ref[...] * (z * _sigmoid(z))).astype(BF16)


def _rope_spread_matrix():
    p = np.zeros((2 * ROPE_HALF, 3 * LANES), np.float32)
    for lane in range(LANES):
        d = lane % HEAD_DIM
        if d < ROPE_DIM:
            p[d % ROPE_HALF, lane] = 1.0
        if d < ROPE_HALF:
            p[ROPE_HALF + d, LANES + lane] = -1.0
        elif d < ROPE_DIM:
            p[ROPE_HALF + d - ROPE_HALF, 2 * LANES + lane] = 1.0
    return jnp.asarray(p, dtype=BF16)


def _attn_layer(x, positions, pre_w, post_w, w_in, b_in, sinks, w_out, b_out):
    B, T, D = x.shape
    tq = min(ATTN_TILE, T)
    assert T % tq == 0 and tq % ATTN_BLOCK == 0 and D == D_MODEL
    inv_freq = ROPE_THETA ** (-(jnp.arange(ROPE_HALF, dtype=F32) * 2.0 / ROPE_DIM))
    const = lambda s: (0, 0)
    nt = T // tq
    n_tiles, cur, prev = _staggered_tiles(B, nt)
    return pl.pallas_call(
        functools.partial(_attn_layer_kernel, nt, n_tiles),
        out_shape=jax.ShapeDtypeStruct((B, T, D), x.dtype),
        grid=(n_tiles + 1,),
        in_specs=[
            pl.BlockSpec(memory_space=pltpu.SMEM),
            pl.BlockSpec((None, tq, D), lambda s: (*cur(s), 0)),
            pl.BlockSpec((None, 1, tq), lambda s: (cur(s)[0], 0, cur(s)[1])),
            pl.BlockSpec((None, tq, D), lambda s: (*prev(s), 0)),
            pl.BlockSpec((ROPE_HALF, 1), const),
            pl.BlockSpec((2 * ROPE_HALF, 3 * LANES), const),
            pl.BlockSpec((1, D), const),
            pl.BlockSpec((1, D), const),
            pl.BlockSpec((D, ATTN_IN), const),
            pl.BlockSpec((1, ATTN_IN), const),
            pl.BlockSpec((ATTN_WIDTH, D), const),
            pl.BlockSpec((1, D), const),
        ],
        out_specs=pl.BlockSpec((None, tq, D), lambda s: (*prev(s), 0)),
        scratch_shapes=[
            pltpu.VMEM((ATTN_BLOCK, KV_WIDTH), F32),
            pltpu.VMEM((ATTN_BLOCK, KV_WIDTH), F32),
            pltpu.VMEM((tq, ATTN_IN), F32),
            pltpu.VMEM((tq, ATTN_WIDTH), F32),
            pltpu.VMEM((tq, ATTN_WIDTH), BF16),
        ],
        compiler_params=pltpu.CompilerParams(
            dimension_semantics=("arbitrary",),
            vmem_limit_bytes=VMEM_LIMIT_BYTES),
        name="swa_layer",
    )(sinks.astype(F32), x, positions.reshape(B, 1, T), x, inv_freq.reshape(ROPE_HALF, 1), _rope_spread_matrix(),
      pre_w.reshape(1, D), post_w.reshape(1, D),
      w_in.astype(BF16), b_in.reshape(1, ATTN_IN), w_out.astype(BF16), b_out.reshape(1, D))


def _boundary_rows(b, m):
    c, n = b.shape
    if 2 * m >= SUBLANES:
        b3 = b.reshape(c // (2 * m), 2 * m, n)
        return jnp.broadcast_to(b3[:, m - 1:m, :], b3.shape).reshape(c, n)
    b3 = b.reshape(c // SUBLANES, SUBLANES, n)
    sub = lax.broadcasted_iota(jnp.int32, b3.shape, 1)
    r = jnp.broadcast_to(b3[:, SUBLANES - m - 1:SUBLANES - m, :], b3.shape)
    for start in range(SUBLANES - 4 * m, -1, -2 * m):
        r = jnp.where(sub < start + 2 * m, jnp.broadcast_to(b3[:, start + m - 1:start + m, :], b3.shape), r)
    return r.reshape(c, n)


def _rec_layer_kernel(layer, chunks_per_seq, n_chunks, x_ref, xprev_ref, lbl_ref, lv_ref, ltri_ref, bsum_ref,
                      prew_ref, postw_ref, win_ref, gnw_ref, wout_ref, o_ref, state_ref, proj_ref, q_ref, k_ref,
                      ghi_ref, glo_ref, b_ref, gated_ref, y_ref):
    s = pl.program_id(0)
    c = jnp.minimum(s, n_chunks - 1) % chunks_per_seq
    chunk = x_ref.shape[0]
    seg = lv_ref.shape[0]
    seg_levels = seg.bit_length() - 1
    base_levels = REC_BASE.bit_length() - 1

    @pl.when(s == 0)
    def _():
        y_ref[...] = jnp.zeros_like(y_ref)

    @pl.when(c == 0)
    def _():
        state_ref[...] = jnp.zeros_like(state_ref)

    x = x_ref[...]
    h = _rmsnorm(x, prew_ref[...]).astype(BF16)

    def project(c0, c1):
        proj_ref[:, c0:c1] = _dot(h, win_ref[:, c0:c1])

    project(FORGET_DIM, 2 * FORGET_DIM)
    project(0, FORGET_DIM)
    project(2 * FORGET_DIM, 2 * FORGET_DIM + REC_WIDTH)

    def forget_gates(hd):
        off = hd * LANES
        f_raw = proj_ref[:, FORGET_DIM + off:FORGET_DIM + off + LANES]

        logits = lbl_ref[:, off:off + LANES]
        ex = jnp.exp(logits - jnp.max(logits, axis=0, keepdims=True))
        lb = jnp.sum(ex[1:layer + 1], axis=0, keepdims=True) / jnp.sum(ex, axis=0, keepdims=True)

        e = jnp.exp(-jnp.abs(f_raw))
        sp = 1.0 / (1.0 + e)
        pos = f_raw >= 0
        sig = jnp.where(pos, sp, e * sp)
        nsig = jnp.where(pos, e * sp, sp)
        log2_f = jnp.log2(lb + (1.0 - lb) * sig)
        k_ref[:, off:off + LANES] = ((1.0 - lb) * nsig).astype(BF16)
        g_hi = log2_f.astype(BF16)
        ghi_ref[:, off:off + LANES] = g_hi
        glo_ref[:, off:off + LANES] = (log2_f - g_hi.astype(F32)).astype(BF16)

    for hd in range(REC_HEADS):
        forget_gates(hd)

    o_ref[...] = xprev_ref[...] + _rmsnorm(y_ref[...], postw_ref[...])

    bsum = bsum_ref[...]
    spread = -jnp.min(_dot(bsum, ghi_ref[...]) + _dot(bsum, glo_ref[...]))

    ltri = ltri_ref[...]
    b_all = _dot(ltri, ghi_ref[...]) + _dot(ltri, glo_ref[...])
    for hd in range(REC_HEADS):
        b_ref[hd] = b_all[:, hd * LANES:(hd + 1) * LANES]
        q_raw = proj_ref[:, hd * LANES:(hd + 1) * LANES]
        q_ref[:, hd * LANES:(hd + 1) * LANES] = (q_raw * _sigmoid(q_raw)).astype(BF16)

    def head_scores(hd, bounded):
        off = hd * LANES
        b = b_ref[hd]
        q = q_ref[:, off:off + LANES]
        k = k_ref[:, off:off + LANES]
        lv = lv_ref[...]

        seg_scores = []
        for s0 in range(0, chunk, seg):
            rows = slice(s0, s0 + seg)
            qs, ks, bs = q[rows], k[rows], b[rows]
            if bounded:
                b3 = bs.reshape(seg // REC_BASE, REC_BASE, LANES)
                mid = 0.5 * (b3[:, 0:1, :] + b3[:, REC_BASE - 1:REC_BASE, :])
                r = jnp.broadcast_to(mid, b3.shape).reshape(seg, LANES)
                a = jnp.where(lv < base_levels,
                              _dot_nt(qs * jnp.exp2(bs - r).astype(BF16), ks * jnp.exp2(r - bs).astype(BF16)), 0.0)
                first_level = base_levels
            else:
                a = jnp.where(lv == -1, _dot_nt(qs, ks), 0.0)
                first_level = 0
            for j in range(first_level, seg_levels):
                en = jnp.exp2(-jnp.abs(bs - _boundary_rows(bs, 1 << j))).astype(BF16)
                a = jnp.where(lv == j, _dot_nt(qs * en, ks * en), a)
            seg_scores.append(a.astype(BF16))
        block_scores = []
        m = seg
        while m < chunk:
            for base in range(0, chunk, 2 * m):
                lo, hi = slice(base, base + m), slice(base + m, base + 2 * m)
                r = b[base + m - 1:base + m]
                qh = q[hi] * jnp.exp2(b[hi] - r).astype(BF16)
                kh = k[lo] * jnp.exp2(r - b[lo]).astype(BF16)
                block_scores.append((base, m, _dot_nt(qh, kh).astype(BF16)))
            m *= 2
        b_last = b[chunk - 1:chunk, :]
        st = state_ref[hd]
        o_inter = _dot(q * jnp.exp2(b).astype(BF16), st.astype(BF16))
        kd = k * jnp.exp2(b_last - b).astype(BF16)
        decay_col = jnp.transpose(jnp.broadcast_to(jnp.exp2(b_last), (SUBLANES, LANES)))[:, 0:1]
        return seg_scores, block_scores, o_inter, kd, st * decay_col

    def head_finish(hd, scores):
        seg_scores, block_scores, o_inter, kd, st_decayed = scores
        off = 2 * FORGET_DIM + hd * LANES
        v = proj_ref[:, off:off + LANES].astype(BF16)
        outs = [_dot(a, v[i * seg:(i + 1) * seg]) for i, a in enumerate(seg_scores)]
        for base, m, a in block_scores:
            upd = _dot(a, v[base:base + m])
            for i in range(m // seg):
                outs[(base + m) // seg + i] += upd[i * seg:(i + 1) * seg]
        o = jnp.concatenate(outs, axis=0) + o_inter
        state_ref[hd] = st_decayed + _dot_tn(kd, v)
        return o * lax.rsqrt(jnp.mean(o * o, axis=-1, keepdims=True) + NORM_EPS) * gnw_ref[...]

    def mix_and_output(bounded):
        z0 = 2 * FORGET_DIM + REC_WIDTH
        z_chunks = [(z0 + i * REC_WIDTH // 4, z0 + (i + 1) * REC_WIDTH // 4) for i in range(4)]
        heads_per_chunk = REC_HEADS // len(z_chunks)
        pending = []

        def gate(upto_chunk):
            while pending and pending[0][0] // heads_per_chunk < upto_chunk:
                hd, o = pending.pop(0)
                z = proj_ref[:, z0 + hd * LANES:z0 + (hd + 1) * LANES]
                gated_ref[:, hd * LANES:(hd + 1) * LANES] = (o * (z * _sigmoid(z))).astype(BF16)

        n_gate_chunks = len(z_chunks)
        ahead = 2
        queue = [head_scores(i, bounded) for i in range(ahead)]
        for hd in range(REC_HEADS):
            scores = queue.pop(0)
            if hd + ahead < REC_HEADS:
                queue.append(head_scores(hd + ahead, bounded))
            pending.append((hd, head_finish(hd, scores)))
            gate(min(hd, n_gate_chunks))
            if z_chunks:
                project(*z_chunks.pop(0))
        gate(n_gate_chunks)
        y_ref[...] = _dot(gated_ref[...], wout_ref[...])

    lax.cond(0.5 * spread <= SAFE_LOG2_EXPONENT, lambda: mix_and_output(True), lambda: mix_and_output(False))


def _level_index(c):
    t = np.arange(c)[:, None]
    s = np.arange(c)[None, :]
    lv = np.floor(np.log2(np.maximum(t ^ s, 1))).astype(np.int32)
    lv = np.where(s < t, lv, np.where(s == t, -1, 99))
    return jnp.asarray(lv, dtype=jnp.int32)


def _rec_layer(x, layer, lb_logits, pre_w, post_w, w_in, gnorm_w, w_out):
    B, T, D = x.shape
    chunk = min(REC_CHUNK, T)
    seg = min(REC_SEG, chunk)
    assert T % chunk == 0 and chunk & (chunk - 1) == 0 and seg >= 2 * SUBLANES and D == D_MODEL
    assert seg % REC_BASE == 0 and chunk % seg == 0
    depth = lb_logits.shape[0]
    ltri = jnp.asarray(np.tril(np.ones((chunk, chunk), np.float32)), dtype=BF16)
    bsum_np = np.zeros((max(SUBLANES, chunk // REC_BASE), chunk), np.float32)
    for i in range(chunk // REC_BASE):
        bsum_np[i, i * REC_BASE + 1:(i + 1) * REC_BASE] = 1.0
    bsum = jnp.asarray(bsum_np, dtype=BF16)
    const = lambda s: (0, 0)
    n_chunks, cur, prev = _staggered_tiles(B, T // chunk)
    return pl.pallas_call(
        functools.partial(_rec_layer_kernel, layer, T // chunk, n_chunks),
        out_shape=jax.ShapeDtypeStruct((B, T, D), x.dtype),
        grid=(n_chunks + 1,),
        in_specs=[
            pl.BlockSpec((None, chunk, D), lambda s: (*cur(s), 0)),
            pl.BlockSpec((None, chunk, D), lambda s: (*prev(s), 0)),
            pl.BlockSpec((depth, FORGET_DIM), const),
            pl.BlockSpec((seg, seg), const),
            pl.BlockSpec((chunk, chunk), const),
            pl.BlockSpec(bsum.shape, const),
            pl.BlockSpec((1, D), const),
            pl.BlockSpec((1, D), const),
            pl.BlockSpec((D, REC_IN), const),
            pl.BlockSpec((1, REC_VALUE_DIM), const),
            pl.BlockSpec((REC_WIDTH, D), const),
        ],
        out_specs=pl.BlockSpec((None, chunk, D), lambda s: (*prev(s), 0)),
        scratch_shapes=[
            pltpu.VMEM((REC_HEADS, REC_KEY_DIM, REC_VALUE_DIM), F32),
            pltpu.VMEM((chunk, REC_IN), F32),
            pltpu.VMEM((chunk, FORGET_DIM), BF16),
            pltpu.VMEM((chunk, FORGET_DIM), BF16),
            pltpu.VMEM((chunk, FORGET_DIM), BF16),
            pltpu.VMEM((chunk, FORGET_DIM), BF16),
            pltpu.VMEM((REC_HEADS, chunk, REC_KEY_DIM), F32),
            pltpu.VMEM((chunk, REC_WIDTH), BF16),
            pltpu.VMEM((chunk, D), F32),
        ],
        compiler_params=pltpu.CompilerParams(
            dimension_semantics=("arbitrary",),
            vmem_limit_bytes=VMEM_LIMIT_BYTES),
        name="hgrn2_layer",
    )(x, x, lb_logits.astype(F32), _level_index(seg), ltri, bsum, pre_w.reshape(1, D), post_w.reshape(1, D),
      w_in.astype(BF16), gnorm_w.reshape(1, REC_VALUE_DIM), w_out.astype(BF16))


def kernel(x, positions, pre_norm_w, post_norm_w, attn_w_in, attn_b_in, attn_sinks, attn_w_out, attn_b_out,
           rec_w_in, rec_lb_logits, rec_gnorm_w, rec_w_out):
    depth = pre_norm_w.shape[0]
    for layer in range(depth):
        j = layer // N_MIXERS
        if layer % N_MIXERS == 0:
            x = _attn_layer(x, positions, pre_norm_w[layer], post_norm_w[layer], attn_w_in[j], attn_b_in[j],
                            attn_sinks[j], attn_w_out[j], attn_b_out[j])
        else:
            x = _rec_layer(x, layer, rec_lb_logits, pre_norm_w[layer], post_norm_w[layer], rec_w_in[j],
                           rec_gnorm_w[j], rec_w_out[j])
    return x
```

```python
import functools

import numpy as np
import jax
import jax.numpy as jnp
from jax import lax
from jax.experimental import pallas as pl
from jax.experimental.pallas import tpu as pltpu

D_MODEL = 1024
N_MIXERS = 2

HEAD_DIM = 64
N_HEADS = D_MODEL // HEAD_DIM
N_KV_HEADS = 2
GROUP = N_HEADS // N_KV_HEADS
ATTN_WIDTH = N_HEADS * HEAD_DIM
KV_WIDTH = N_KV_HEADS * HEAD_DIM
ATTN_IN = 2 * ATTN_WIDTH + 2 * KV_WIDTH
ATTN_BLOCK = 128
ROPE_THETA = 500000.0
ROPE_DIM = HEAD_DIM // 4
ROPE_HALF = ROPE_DIM // 2

REC_HEADS = 8
REC_KEY_DIM = 128
REC_VALUE_DIM = D_MODEL // REC_HEADS
FORGET_DIM = REC_HEADS * REC_KEY_DIM
REC_WIDTH = REC_HEADS * REC_VALUE_DIM
REC_IN = 2 * FORGET_DIM + 2 * REC_WIDTH

NORM_EPS = 1e-6
LOG2_E = 1.4426950408889634

LANES = 128
SUBLANES = 8
ATTN_TILE = 512
REC_CHUNK = 256
REC_SEG = 128
REC_BASE = 128
SAFE_LOG2_EXPONENT = 112.0
VMEM_LIMIT_BYTES = 48 * 1024 * 1024

F32 = jnp.float32
BF16 = jnp.bfloat16


def _staggered_tiles(batch, tiles_per_seq):
    n = batch * tiles_per_seq

    def cur(s):
        c = jnp.minimum(s, n - 1)
        return c // tiles_per_seq, c % tiles_per_seq

    def prev(s):
        p = jnp.maximum(s - 1, 0)
        return p // tiles_per_seq, p % tiles_per_seq

    return n, cur, prev


def _rmsnorm(x, w):
    return x * lax.rsqrt(jnp.mean(x * x, axis=-1, keepdims=True) + NORM_EPS) * w


def _sigmoid(x):
    return 1.0 / (1.0 + jnp.exp(-x))


def _dot(a, b):
    return jnp.dot(a, b, preferred_element_type=F32)


def _dot_nt(a, b):
    return lax.dot_general(a, b, (((1,), (1,)), ((), ())), preferred_element_type=F32)


def _dot_tn(a, b):
    return lax.dot_general(a, b, (((0,), (0,)), ((), ())), preferred_element_type=F32)


def _split_dot_tn(x, p):
    acc = None
    for _ in range(3):
        part = x.astype(BF16)
        x = x - part.astype(F32)
        term = _dot_tn(part, p)
        acc = term if acc is None else acc + term
    return acc


def _attn_layer_kernel(tiles_per_seq, n_tiles, sinks_ref, x_ref, pos_ref, xprev_ref, invf_ref, spread_ref,
                       prew_ref, postw_ref, win_ref, bin_ref, wout_ref, bout_ref, o_ref,
                       kprev_ref, vprev_ref, proj_ref, att_ref, gated_ref):
    s = pl.program_id(0)
    t = jnp.minimum(s, n_tiles - 1) % tiles_per_seq
    tq = x_ref.shape[0]
    n_blocks = tq // ATTN_BLOCK

    @pl.when(s == 0)
    def _():
        gated_ref[...] = jnp.zeros_like(gated_ref)

    @pl.when(t == 0)
    def _():
        kprev_ref[...] = jnp.zeros_like(kprev_ref)
        vprev_ref[...] = jnp.zeros_like(vprev_ref)

    y_prev = _dot(gated_ref[...], wout_ref[...]) + bout_ref[...]

    ang_t = invf_ref[...] * pos_ref[...].astype(F32)
    trig_t = jnp.concatenate([jnp.cos(ang_t), jnp.sin(ang_t)], axis=0)
    tabs = _split_dot_tn(trig_t, spread_ref[...])
    lane = lax.broadcasted_iota(jnp.int32, (1, LANES), 1)
    d = lane % HEAD_DIM
    c_tab = tabs[:, :LANES] + jnp.where(d < ROPE_DIM, 0.0, 1.0)
    s_lo = tabs[:, LANES:2 * LANES]
    s_hi = tabs[:, 2 * LANES:]

    x = x_ref[...]
    h = _rmsnorm(x, prew_ref[...]).astype(BF16)

    def project(c0, c1):
        proj_ref[:, c0:c1] = _dot(h, win_ref[:, c0:c1]) + bin_ref[:, c0:c1]

    q_cols = GROUP * HEAD_DIM
    z0 = ATTN_WIDTH + 2 * KV_WIDTH
    project(ATTN_WIDTH, z0)
    project(0, q_cols)
    project(q_cols, 2 * q_cols)
    later_chunks = [(g * q_cols, (g + 1) * q_cols) for g in range(2, N_KV_HEADS)]
    later_chunks += [(z0 + i * ATTN_WIDTH // 4, z0 + (i + 1) * ATTN_WIDTH // 4) for i in range(4)]

    o_ref[...] = xprev_ref[...] + _rmsnorm(y_prev, postw_ref[...])

    def rope(xc, rows=slice(None)):
        return (xc * c_tab[rows] + pltpu.roll(xc, LANES - ROPE_HALF, 1) * s_lo[rows]
                + pltpu.roll(xc, ROPE_HALF, 1) * s_hi[rows])

    k_all = rope(proj_ref[:, ATTN_WIDTH:ATTN_WIDTH + KV_WIDTH])
    v_all = proj_ref[:, ATTN_WIDTH + KV_WIDTH:ATTN_WIDTH + 2 * KV_WIDTH]

    lo_half = lane < HEAD_DIM
    from_prev = (lax.broadcasted_iota(jnp.int32, (ATTN_BLOCK, ATTN_BLOCK), 1)
                 > lax.broadcasted_iota(jnp.int32, (ATTN_BLOCK, ATTN_BLOCK), 0))
    no_prev = jnp.where(t == 0, -jnp.inf, 0.0)
    scale = HEAD_DIM ** -0.5 * LOG2_E

    def block_diag(a):
        a_sw = pltpu.roll(a, HEAD_DIM, 1)
        zero = jnp.zeros_like(a)
        g0 = jnp.concatenate([jnp.where(lo_half, a, zero), jnp.where(lo_half, zero, a_sw)], axis=0)
        g1 = jnp.concatenate([jnp.where(lo_half, a_sw, zero), jnp.where(lo_half, zero, a)], axis=0)
        return g0.astype(BF16), g1.astype(BF16)

    k_prev = kprev_ref[...]
    v_prev = vprev_ref[...]
    kbs, vbs = [], []
    for j in range(n_blocks):
        rows = slice(j * ATTN_BLOCK, (j + 1) * ATTN_BLOCK)
        k_cur, v_cur = k_all[rows], v_all[rows]
        kbs.append(block_diag(jnp.concatenate([k_prev, k_cur], axis=0)))
        vbs.append(block_diag(jnp.concatenate([v_prev, v_cur], axis=0)))
        k_prev, v_prev = k_cur, v_cur
    kprev_ref[...] = k_prev
    vprev_ref[...] = v_prev

    pairs_per_group = GROUP // 2

    def scores(j, g):
        rows = slice(j * ATTN_BLOCK, (j + 1) * ATTN_BLOCK)
        qs = [(rope(proj_ref[rows, p * LANES:(p + 1) * LANES], rows) * scale).astype(BF16)
              for p in range(g * pairs_per_group, (g + 1) * pairs_per_group)]
        return _dot_nt(jnp.concatenate(qs, axis=0), kbs[j][g])

    items = [(j, g) for g in range(N_KV_HEADS) for j in range(n_blocks)]
    s_next = scores(*items[0])
    for i, (j, g) in enumerate(items):
        s_group = s_next
        if later_chunks and i % 2 == 0:
            project(*later_chunks.pop(0))
        if i + 1 < len(items):
            s_next = scores(*items[i + 1])
        probs, invs = [], []
        for pi in range(pairs_per_group):
            p = g * pairs_per_group + pi
            s = s_group[pi * ATTN_BLOCK:(pi + 1) * ATTN_BLOCK]
            halves, inv = [], []
            for e in range(2):
                s_prev = s[:, 2 * e * ATTN_BLOCK:(2 * e + 1) * ATTN_BLOCK]
                s_cur = s[:, (2 * e + 1) * ATTN_BLOCK:(2 * e + 2) * ATTN_BLOCK]
                if j == 0:
                    s_prev = s_prev + no_prev
                sw = jnp.where(from_prev, s_prev, s_cur)
                sink = sinks_ref[2 * p + e] * LOG2_E
                m = jnp.maximum(jnp.max(sw, axis=-1, keepdims=True), sink)
                pw = jnp.exp2(sw - m)
                denom = jnp.sum(pw, axis=-1, keepdims=True) + jnp.exp2(sink - m)
                pw = pw.astype(BF16)
                zero = jnp.zeros_like(pw)
                halves += [jnp.where(from_prev, pw, zero), jnp.where(from_prev, zero, pw)]
                inv.append(1.0 / denom)
            probs.append(jnp.concatenate(halves, axis=1))
            invs.append(jnp.where(lo_half, inv[0], inv[1]))
        o = _dot(jnp.concatenate(probs, axis=0), vbs[j][g])
        for pi in range(pairs_per_group):
            p = g * pairs_per_group + pi
            att_ref[j * ATTN_BLOCK:(j + 1) * ATTN_BLOCK, p * LANES:(p + 1) * LANES] = (
                o[pi * ATTN_BLOCK:(pi + 1) * ATTN_BLOCK] * invs[pi])
    for chunk in later_chunks:
        project(*chunk)

    z = proj_ref[:, z0:]
    gated_ref[...] = (att_ref[...] * (z * _sigmoid(z))).astype(BF16)


def _rope_spread_matrix():
    p = np.zeros((2 * ROPE_HALF, 3 * LANES), np.float32)
    for lane in range(LANES):
        d = lane % HEAD_DIM
        if d < ROPE_DIM:
            p[d % ROPE_HALF, lane] = 1.0
        if d < ROPE_HALF:
            p[ROPE_HALF + d, LANES + lane] = -1.0
        elif d < ROPE_DIM:
            p[ROPE_HALF + d - ROPE_HALF, 2 * LANES + lane] = 1.0
    return jnp.asarray(p, dtype=BF16)


def _attn_layer(x, positions, pre_w, post_w, w_in, b_in, sinks, w_out, b_out):
    B, T, D = x.shape
    tq = min(ATTN_TILE, T)
    assert T % tq == 0 and tq % ATTN_BLOCK == 0 and D == D_MODEL
    inv_freq = ROPE_THETA ** (-(jnp.arange(ROPE_HALF, dtype=F32) * 2.0 / ROPE_DIM))
    const = lambda s: (0, 0)
    nt = T // tq
    n_tiles, cur, prev = _staggered_tiles(B, nt)
    return pl.pallas_call(
        functools.partial(_attn_layer_kernel, nt, n_tiles),
        out_shape=jax.ShapeDtypeStruct((B, T, D), x.dtype),
        grid=(n_tiles + 1,),
        in_specs=[
            pl.BlockSpec(memory_space=pltpu.SMEM),
            pl.BlockSpec((None, tq, D), lambda s: (*cur(s), 0)),
            pl.BlockSpec((None, 1, tq), lambda s: (cur(s)[0], 0, cur(s)[1])),
            pl.BlockSpec((None, tq, D), lambda s: (*prev(s), 0)),
            pl.BlockSpec((ROPE_HALF, 1), const),
            pl.BlockSpec((2 * ROPE_HALF, 3 * LANES), const),
            pl.BlockSpec((1, D), const),
            pl.BlockSpec((1, D), const),
            pl.BlockSpec((D, ATTN_IN), const),
            pl.BlockSpec((1, ATTN_IN), const),
            pl.BlockSpec((ATTN_WIDTH, D), const),
            pl.BlockSpec((1, D), const),
        ],
        out_specs=pl.BlockSpec((None, tq, D), lambda s: (*prev(s), 0)),
        scratch_shapes=[
            pltpu.VMEM((ATTN_BLOCK, KV_WIDTH), F32),
            pltpu.VMEM((ATTN_BLOCK, KV_WIDTH), F32),
            pltpu.VMEM((tq, ATTN_IN), F32),
            pltpu.VMEM((tq, ATTN_WIDTH), F32),
            pltpu.VMEM((tq, ATTN_WIDTH), BF16),
        ],
        compiler_params=pltpu.CompilerParams(
            dimension_semantics=("arbitrary",),
            vmem_limit_bytes=VMEM_LIMIT_BYTES),
        name="swa_layer",
    )(sinks.astype(F32), x, positions.reshape(B, 1, T), x, inv_freq.reshape(ROPE_HALF, 1), _rope_spread_matrix(),
      pre_w.reshape(1, D), post_w.reshape(1, D),
      w_in.astype(BF16), b_in.reshape(1, ATTN_IN), w_out.astype(BF16), b_out.reshape(1, D))


def _boundary_rows(b, m):
    c, n = b.shape
    if 2 * m >= SUBLANES:
        b3 = b.reshape(c // (2 * m), 2 * m, n)
        return jnp.broadcast_to(b3[:, m - 1:m, :], b3.shape).reshape(c, n)
    b3 = b.reshape(c // SUBLANES, SUBLANES, n)
    sub = lax.broadcasted_iota(jnp.int32, b3.shape, 1)
    r = jnp.broadcast_to(b3[:, SUBLANES - m - 1:SUBLANES - m, :], b3.shape)
    for start in range(SUBLANES - 4 * m, -1, -2 * m):
        r = jnp.where(sub < start + 2 * m, jnp.broadcast_to(b3[:, start + m - 1:start + m, :], b3.shape), r)
    return r.reshape(c, n)


def _rec_layer_kernel(layer, chunks_per_seq, n_chunks, x_ref, xprev_ref, lbl_ref, lv_ref, ltri_ref, bsum_ref,
                      prew_ref, postw_ref, win_ref, gnw_ref, wout_ref, o_ref, state_ref, proj_ref, q_ref, k_ref,
                      ghi_ref, glo_ref, b_ref, gated_ref, y_ref):
    s = pl.program_id(0)
    c = jnp.minimum(s, n_chunks - 1) % chunks_per_seq
    chunk = x_ref.shape[0]
    seg = lv_ref.shape[0]
    seg_levels = seg.bit_length() - 1
    base_levels = REC_BASE.bit_length() - 1

    @pl.when(s == 0)
    def _():
        y_ref[...] = jnp.zeros_like(y_ref)

    @pl.when(c == 0)
    def _():
        state_ref[...] = jnp.zeros_like(state_ref)

    x = x_ref[...]
    h = _rmsnorm(x, prew_ref[...]).astype(BF16)

    def project(c0, c1):
        proj_ref[:, c0:c1] = _dot(h, win_ref[:, c0:c1])

    project(FORGET_DIM, 2 * FORGET_DIM)
    project(0, FORGET_DIM)
    project(2 * FORGET_DIM, 2 * FORGET_DIM + REC_WIDTH)

    def forget_gates(hd):
        off = hd * LANES
        f_raw = proj_ref[:, FORGET_DIM + off:FORGET_DIM + off + LANES]

        logits = lbl_ref[:, off:off + LANES]
        ex = jnp.exp(logits - jnp.max(logits, axis=0, keepdims=True))
        lb = jnp.sum(ex[1:layer + 1], axis=0, keepdims=True) / jnp.sum(ex, axis=0, keepdims=True)

        e = jnp.exp(-jnp.abs(f_raw))
        sp = 1.0 / (1.0 + e)
        pos = f_raw >= 0
        sig = jnp.where(pos, sp, e * sp)
        nsig = jnp.where(pos, e * sp, sp)
        log2_f = jnp.log2(lb + (1.0 - lb) * sig)
        k_ref[:, off:off + LANES] = ((1.0 - lb) * nsig).astype(BF16)
        g_hi = log2_f.astype(BF16)
        ghi_ref[:, off:off + LANES] = g_hi
        glo_ref[:, off:off + LANES] = (log2_f - g_hi.astype(F32)).astype(BF16)

    for hd in range(REC_HEADS):
        forget_gates(hd)

    o_ref[...] = xprev_ref[...] + _rmsnorm(y_ref[...], postw_ref[...])

    bsum = bsum_ref[...]
    spread = -jnp.min(_dot(bsum, ghi_ref[...]) + _dot(bsum, glo_ref[...]))

    ltri = ltri_ref[...]
    b_all = _dot(ltri, ghi_ref[...]) + _dot(ltri, glo_ref[...])
    for hd in range(REC_HEADS):
        b_ref[hd] = b_all[:, hd * LANES:(hd + 1) * LANES]
        q_raw = proj_ref[:, hd * LANES:(hd + 1) * LANES]
        q_ref[:, hd * LANES:(hd + 1) * LANES] = (q_raw * _sigmoid(q_raw)).astype(BF16)

    def head_scores(hd, bounded):
        off = hd * LANES
        b = b_ref[hd]
        q = q_ref[:, off:off + LANES]
        k = k_ref[:, off:off + LANES]
        lv = lv_ref[...]

        seg_scores = []
        for s0 in range(0, chunk, seg):
            rows = slice(s0, s0 + seg)
            qs, ks, bs = q[rows], k[rows], b[rows]
            if bounded:
                b3 = bs.reshape(seg // REC_BASE, REC_BASE, LANES)
                mid = 0.5 * (b3[:, 0:1, :] + b3[:, REC_BASE - 1:REC_BASE, :])
                r = jnp.broadcast_to(mid, b3.shape).reshape(seg, LANES)
                a = jnp.where(lv < base_levels,
                              _dot_nt(qs * jnp.exp2(bs - r).astype(BF16), ks * jnp.exp2(r - bs).astype(BF16)), 0.0)
                first_level = base_levels
            else:
                a = jnp.where(lv == -1, _dot_nt(qs, ks), 0.0)
                first_level = 0
            for j in range(first_level, seg_levels):
                en = jnp.exp2(-jnp.abs(bs - _boundary_rows(bs, 1 << j))).astype(BF16)
                a = jnp.where(lv == j, _dot_nt(qs * en, ks * en), a)
            seg_scores.append(a.astype(BF16))
        block_scores = []
        m = seg
        while m < chunk:
            for base in range(0, chunk, 2 * m):
                lo, hi = slice(base, base + m), slice(base + m, base + 2 * m)
                r = b[base + m - 1:base + m]
                qh = q[hi] * jnp.exp2(b[hi] - r).astype(BF16)
                kh = k[lo] * jnp.exp2(r - b[lo]).astype(BF16)
                block_scores.append((base, m, _dot_nt(qh, kh).astype(BF16)))
            m *= 2
        b_last = b[chunk - 1:chunk, :]
        st = state_ref[hd]
        o_inter = _dot_nt(q * jnp.exp2(b).astype(BF16), st.astype(BF16))
        kd = k * jnp.exp2(b_last - b).astype(BF16)
        return seg_scores, block_scores, o_inter, kd, st * jnp.exp2(b_last)

    def head_finish(hd, scores):
        seg_scores, block_scores, o_inter, kd, st_decayed = scores
        off = 2 * FORGET_DIM + hd * LANES
        v = proj_ref[:, off:off + LANES].astype(BF16)
        outs = [_dot(a, v[i * seg:(i + 1) * seg]) for i, a in enumerate(seg_scores)]
        for base, m, a in block_scores:
            upd = _dot(a, v[base:base + m])
            for i in range(m // seg):
                outs[(base + m) // seg + i] += upd[i * seg:(i + 1) * seg]
        o = jnp.concatenate(outs, axis=0) + o_inter
        state_ref[hd] = st_decayed + _dot_tn(v, kd)
        return o * lax.rsqrt(jnp.mean(o * o, axis=-1, keepdims=True) + NORM_EPS) * gnw_ref[...]

    def mix_and_output(bounded):
        z0 = 2 * FORGET_DIM + REC_WIDTH
        z_chunks = [(z0 + i * REC_WIDTH // 4, z0 + (i + 1) * REC_WIDTH // 4) for i in range(4)]
        heads_per_chunk = REC_HEADS // len(z_chunks)
        pending = []

        def gate(upto_chunk):
            while pending and pending[0][0] // heads_per_chunk < upto_chunk:
                hd, o = pending.pop(0)
                z = proj_ref[:, z0 + hd * LANES:z0 + (hd + 1) * LANES]
                gated_ref[:, hd * LANES:(hd + 1) * LANES] = (o * (z * _sigmoid(z))).astype(BF16)

        n_gate_chunks = len(z_chunks)
        ahead = 2
        queue = [head_scores(i, bounded) for i in range(ahead)]
        for hd in range(REC_HEADS):
            scores = queue.pop(0)
            if hd + ahead < REC_HEADS:
                queue.append(head_scores(hd + ahead, bounded))
            pending.append((hd, head_finish(hd, scores)))
            gate(min(hd, n_gate_chunks))
            if z_chunks:
                project(*z_chunks.pop(0))
        gate(n_gate_chunks)
        y_ref[...] = _dot(gated_ref[...], wout_ref[...])

    lax.cond(0.5 * spread <= SAFE_LOG2_EXPONENT, lambda: mix_and_output(True), lambda: mix_and_output(False))


def _level_index(c):
    t = np.arange(c)[:, None]
    s = np.arange(c)[None, :]
    lv = np.floor(np.log2(np.maximum(t ^ s, 1))).astype(np.int32)
    lv = np.where(s < t, lv, np.where(s == t, -1, 99))
    return jnp.asarray(lv, dtype=jnp.int32)


def _rec_layer(x, layer, lb_logits, pre_w, post_w, w_in, gnorm_w, w_out):
    B, T, D = x.shape
    chunk = min(REC_CHUNK, T)
    seg = min(REC_SEG, chunk)
    assert T % chunk == 0 and chunk & (chunk - 1) == 0 and seg >= 2 * SUBLANES and D == D_MODEL
    depth = lb_logits.shape[0]
    ltri = jnp.asarray(np.tril(np.ones((chunk, chunk), np.float32)), dtype=BF16)
    bsum_np = np.zeros((max(SUBLANES, chunk // REC_BASE), chunk), np.float32)
    for i in range(chunk // REC_BASE):
        bsum_np[i, i * REC_BASE + 1:(i + 1) * REC_BASE] = 1.0
    bsum = jnp.asarray(bsum_np, dtype=BF16)
    const = lambda s: (0, 0)
    n_chunks, cur, prev = _staggered_tiles(B, T // chunk)
    return pl.pallas_call(
        functools.partial(_rec_layer_kernel, layer, T // chunk, n_chunks),
        out_shape=jax.ShapeDtypeStruct((B, T, D), x.dtype),
        grid=(n_chunks + 1,),
        in_specs=[
            pl.BlockSpec((None, chunk, D), lambda s: (*cur(s), 0)),
            pl.BlockSpec((None, chunk, D), lambda s: (*prev(s), 0)),
            pl.BlockSpec((depth, FORGET_DIM), const),
            pl.BlockSpec((seg, seg), const),
            pl.BlockSpec((chunk, chunk), const),
            pl.BlockSpec(bsum.shape, const),
            pl.BlockSpec((1, D), const),
            pl.BlockSpec((1, D), const),
            pl.BlockSpec((D, REC_IN), const),
            pl.BlockSpec((1, REC_VALUE_DIM), const),
            pl.BlockSpec((REC_WIDTH, D), const),
        ],
        out_specs=pl.BlockSpec((None, chunk, D), lambda s: (*prev(s), 0)),
        scratch_shapes=[
            pltpu.VMEM((REC_HEADS, REC_VALUE_DIM, REC_KEY_DIM), F32),
            pltpu.VMEM((chunk, REC_IN), F32),
            pltpu.VMEM((chunk, FORGET_DIM), BF16),
            pltpu.VMEM((chunk, FORGET_DIM), BF16),
            pltpu.VMEM((chunk, FORGET_DIM), BF16),
            pltpu.VMEM((chunk, FORGET_DIM), BF16),
            pltpu.VMEM((REC_HEADS, chunk, REC_KEY_DIM), F32),
            pltpu.VMEM((chunk, REC_WIDTH), BF16),
            pltpu.VMEM((chunk, D), F32),
        ],
        compiler_params=pltpu.CompilerParams(
            dimension_semantics=("arbitrary",),
            vmem_limit_bytes=VMEM_LIMIT_BYTES),
        name="hgrn2_layer",
    )(x, x, lb_logits.astype(F32), _level_index(seg), ltri, bsum, pre_w.reshape(1, D), post_w.reshape(1, D),
      w_in.astype(BF16), gnorm_w.reshape(1, REC_VALUE_DIM), w_out.astype(BF16))


def kernel(x, positions, pre_norm_w, post_norm_w, attn_w_in, attn_b_in, attn_sinks, attn_w_out, attn_b_out,
           rec_w_in, rec_lb_logits, rec_gnorm_w, rec_w_out):
    depth = pre_norm_w.shape[0]
    for layer in range(depth):
        j = layer // N_MIXERS
        if layer % N_MIXERS == 0:
            x = _attn_layer(x, positions, pre_norm_w[layer], post_norm_w[layer], attn_w_in[j], attn_b_in[j],
                            attn_sinks[j], attn_w_out[j], attn_b_out[j])
        else:
            x = _rec_layer(x, layer, rec_lb_logits, pre_norm_w[layer], post_norm_w[layer], rec_w_in[j],
                           rec_gnorm_w[j], rec_w_out[j])
    return x
```

```python
import functools

import numpy as np
import jax
import jax.numpy as jnp
from jax import lax
from jax.experimental import pallas as pl
from jax.experimental.pallas import tpu as pltpu

D_MODEL = 1024
N_MIXERS = 2

HEAD_DIM = 64
N_HEADS = D_MODEL // HEAD_DIM
N_KV_HEADS = 2
GROUP = N_HEADS // N_KV_HEADS
ATTN_WIDTH = N_HEADS * HEAD_DIM
KV_WIDTH = N_KV_HEADS * HEAD_DIM
ATTN_IN = 2 * ATTN_WIDTH + 2 * KV_WIDTH
ATTN_BLOCK = 128
ROPE_THETA = 500000.0
ROPE_DIM = HEAD_DIM // 4
ROPE_HALF = ROPE_DIM // 2

REC_HEADS = 8
REC_KEY_DIM = 128
REC_VALUE_DIM = D_MODEL // REC_HEADS
FORGET_DIM = REC_HEADS * REC_KEY_DIM
REC_WIDTH = REC_HEADS * REC_VALUE_DIM
REC_IN = 2 * FORGET_DIM + 2 * REC_WIDTH

NORM_EPS = 1e-6
LOG2_E = 1.4426950408889634

LANES = 128
SUBLANES = 8
ATTN_TILE = 512
REC_CHUNK = 256
REC_SEG = 128
REC_BASE = 128
SAFE_LOG2_EXPONENT = 112.0
VMEM_LIMIT_BYTES = 48 * 1024 * 1024

F32 = jnp.float32
BF16 = jnp.bfloat16


def _staggered_tiles(batch, tiles_per_seq):
    n = batch * tiles_per_seq

    def cur(s):
        c = jnp.minimum(s, n - 1)
        return c // tiles_per_seq, c % tiles_per_seq

    def prev(s):
        p = jnp.maximum(s - 1, 0)
        return p // tiles_per_seq, p % tiles_per_seq

    return n, cur, prev


def _rmsnorm(x, w):
    return x * lax.rsqrt(jnp.mean(x * x, axis=-1, keepdims=True) + NORM_EPS) * w


def _sigmoid(x):
    return 1.0 / (1.0 + jnp.exp(-x))


def _dot(a, b):
    return jnp.dot(a, b, preferred_element_type=F32)


def _dot_nt(a, b):
    return lax.dot_general(a, b, (((1,), (1,)), ((), ())), preferred_element_type=F32)


def _dot_tn(a, b):
    return lax.dot_general(a, b, (((0,), (0,)), ((), ())), preferred_element_type=F32)


def _split_dot_tn(x, p):
    acc = None
    for _ in range(3):
        part = x.astype(BF16)
        x = x - part.astype(F32)
        term = _dot_tn(part, p)
        acc = term if acc is None else acc + term
    return acc


def _attn_layer_kernel(tiles_per_seq, n_tiles, sinks_ref, x_ref, pos_ref, xprev_ref, invf_ref, spread_ref,
                       prew_ref, postw_ref, win_ref, bin_ref, wout_ref, bout_ref, o_ref,
                       kprev_ref, vprev_ref, proj_ref, att_ref, gated_ref):
    s = pl.program_id(0)
    t = jnp.minimum(s, n_tiles - 1) % tiles_per_seq
    tq = x_ref.shape[0]
    n_blocks = tq // ATTN_BLOCK

    @pl.when(s == 0)
    def _():
        gated_ref[...] = jnp.zeros_like(gated_ref)

    @pl.when(t == 0)
    def _():
        kprev_ref[...] = jnp.zeros_like(kprev_ref)
        vprev_ref[...] = jnp.zeros_like(vprev_ref)

    y_prev = _dot(gated_ref[...], wout_ref[...]) + bout_ref[...]

    ang_t = invf_ref[...] * pos_ref[...].astype(F32)
    trig_t = jnp.concatenate([jnp.cos(ang_t), jnp.sin(ang_t)], axis=0)
    tabs = _split_dot_tn(trig_t, spread_ref[...])
    lane = lax.broadcasted_iota(jnp.int32, (1, LANES), 1)
    d = lane % HEAD_DIM
    c_tab = tabs[:, :LANES] + jnp.where(d < ROPE_DIM, 0.0, 1.0)
    s_lo = tabs[:, LANES:2 * LANES]
    s_hi = tabs[:, 2 * LANES:]

    x = x_ref[...]
    h = _rmsnorm(x, prew_ref[...]).astype(BF16)

    def project(c0, c1):
        proj_ref[:, c0:c1] = _dot(h, win_ref[:, c0:c1]) + bin_ref[:, c0:c1]

    q_cols = GROUP * HEAD_DIM
    z0 = ATTN_WIDTH + 2 * KV_WIDTH
    project(ATTN_WIDTH, z0)
    project(0, q_cols)
    project(q_cols, 2 * q_cols)
    later_chunks = [(g * q_cols, (g + 1) * q_cols) for g in range(2, N_KV_HEADS)]
    later_chunks += [(z0 + i * ATTN_WIDTH // 4, z0 + (i + 1) * ATTN_WIDTH // 4) for i in range(4)]

    o_ref[...] = xprev_ref[...] + _rmsnorm(y_prev, postw_ref[...])

    def rope(xc, rows=slice(None)):
        return (xc * c_tab[rows] + pltpu.roll(xc, LANES - ROPE_HALF, 1) * s_lo[rows]
                + pltpu.roll(xc, ROPE_HALF, 1) * s_hi[rows])

    k_all = rope(proj_ref[:, ATTN_WIDTH:ATTN_WIDTH + KV_WIDTH])
    v_all = proj_ref[:, ATTN_WIDTH + KV_WIDTH:ATTN_WIDTH + 2 * KV_WIDTH]

    lo_half = lane < HEAD_DIM
    from_prev = (lax.broadcasted_iota(jnp.int32, (ATTN_BLOCK, ATTN_BLOCK), 1)
                 > lax.broadcasted_iota(jnp.int32, (ATTN_BLOCK, ATTN_BLOCK), 0))
    no_prev = jnp.where(t == 0, -jnp.inf, 0.0)
    scale = HEAD_DIM ** -0.5 * LOG2_E

    def block_diag(a):
        a_sw = pltpu.roll(a, HEAD_DIM, 1)
        zero = jnp.zeros_like(a)
        g0 = jnp.concatenate([jnp.where(lo_half, a, zero), jnp.where(lo_half, zero, a_sw)], axis=0)
        g1 = jnp.concatenate([jnp.where(lo_half, a_sw, zero), jnp.where(lo_half, zero, a)], axis=0)
        return g0.astype(BF16), g1.astype(BF16)

    k_prev = kprev_ref[...]
    v_prev = vprev_ref[...]
    kbs, vbs = [], []
    for j in range(n_blocks):
        rows = slice(j * ATTN_BLOCK, (j + 1) * ATTN_BLOCK)
        k_cur, v_cur = k_all[rows], v_all[rows]
        kbs.append(block_diag(jnp.concatenate([k_prev, k_cur], axis=0)))
        vbs.append(block_diag(jnp.concatenate([v_prev, v_cur], axis=0)))
        k_prev, v_prev = k_cur, v_cur
    kprev_ref[...] = k_prev
    vprev_ref[...] = v_prev

    pairs_per_group = GROUP // 2

    def scores(j, g):
        rows = slice(j * ATTN_BLOCK, (j + 1) * ATTN_BLOCK)
        qs = [(rope(proj_ref[rows, p * LANES:(p + 1) * LANES], rows) * scale).astype(BF16)
              for p in range(g * pairs_per_group, (g + 1) * pairs_per_group)]
        return _dot_nt(jnp.concatenate(qs, axis=0), kbs[j][g])

    items = [(j, g) for g in range(N_KV_HEADS) for j in range(n_blocks)]
    s_next = scores(*items[0])
    for i, (j, g) in enumerate(items):
        s_group = s_next
        if later_chunks and i % 2 == 0:
            project(*later_chunks.pop(0))
        if i + 1 < len(items):
            s_next = scores(*items[i + 1])
        probs, invs = [], []
        for pi in range(pairs_per_group):
            p = g * pairs_per_group + pi
            s = s_group[pi * ATTN_BLOCK:(pi + 1) * ATTN_BLOCK]
            halves, inv = [], []
            for e in range(2):
                s_prev = s[:, 2 * e * ATTN_BLOCK:(2 * e + 1) * ATTN_BLOCK]
                s_cur = s[:, (2 * e + 1) * ATTN_BLOCK:(2 * e + 2) * ATTN_BLOCK]
                if j == 0:
                    s_prev = s_prev + no_prev
                sw = jnp.where(from_prev, s_prev, s_cur)
                sink = sinks_ref[2 * p + e] * LOG2_E
                m = jnp.maximum(jnp.max(sw, axis=-1, keepdims=True), sink)
                pw = jnp.exp2(sw - m)
                denom = jnp.sum(pw, axis=-1, keepdims=True) + jnp.exp2(sink - m)
                pw = pw.astype(BF16)
                zero = jnp.zeros_like(pw)
                halves += [jnp.where(from_prev, pw, zero), jnp.where(from_prev, zero, pw)]
                inv.append(1.0 / denom)
            probs.append(jnp.concatenate(halves, axis=1))
            invs.append(jnp.where(lo_half, inv[0], inv[1]))
        o = _dot(jnp.concatenate(probs, axis=0), vbs[j][g])
        for pi in range(pairs_per_group):
            p = g * pairs_per_group + pi
            att_ref[j * ATTN_BLOCK:(j + 1) * ATTN_BLOCK, p * LANES:(p + 1) * LANES] = (
                o[pi * ATTN_BLOCK:(pi + 1) * ATTN_BLOCK] * invs[pi])
    for chunk in later_chunks:
        project(*chunk)

    z = proj_ref[:, z0:]
    gated_ref[...] = (att_ref[...] * (z * _sigmoid(z))).astype(BF16)


def _rope_spread_matrix():
    p = np.zeros((2 * ROPE_HALF, 3 * LANES), np.float32)
    for lane in range(LANES):
        d = lane % HEAD_DIM
        if d < ROPE_DIM:
            p[d % ROPE_HALF, lane] = 1.0
        if d < ROPE_HALF:
            p[ROPE_HALF + d, LANES + lane] = -1.0
        elif d < ROPE_DIM:
            p[ROPE_HALF + d - ROPE_HALF, 2 * LANES + lane] = 1.0
    return jnp.asarray(p, dtype=BF16)


def _attn_layer(x, positions, pre_w, post_w, w_in, b_in, sinks, w_out, b_out):
    B, T, D = x.shape
    tq = min(ATTN_TILE, T)
    assert T % tq == 0 and tq % ATTN_BLOCK == 0 and D == D_MODEL
    inv_freq = ROPE_THETA ** (-(jnp.arange(ROPE_HALF, dtype=F32) * 2.0 / ROPE_DIM))
    const = lambda s: (0, 0)
    nt = T // tq
    n_tiles, cur, prev = _staggered_tiles(B, nt)
    return pl.pallas_call(
        functools.partial(_attn_layer_kernel, nt, n_tiles),
        out_shape=jax.ShapeDtypeStruct((B, T, D), x.dtype),
        grid=(n_tiles + 1,),
        in_specs=[
            pl.BlockSpec(memory_space=pltpu.SMEM),
            pl.BlockSpec((None, tq, D), lambda s: (*cur(s), 0)),
            pl.BlockSpec((None, 1, tq), lambda s: (cur(s)[0], 0, cur(s)[1])),
            pl.BlockSpec((None, tq, D), lambda s: (*prev(s), 0)),
            pl.BlockSpec((ROPE_HALF, 1), const),
            pl.BlockSpec((2 * ROPE_HALF, 3 * LANES), const),
            pl.BlockSpec((1, D), const),
            pl.BlockSpec((1, D), const),
            pl.BlockSpec((D, ATTN_IN), const),
            pl.BlockSpec((1, ATTN_IN), const),
            pl.BlockSpec((ATTN_WIDTH, D), const),
            pl.BlockSpec((1, D), const),
        ],
        out_specs=pl.BlockSpec((None, tq, D), lambda s: (*prev(s), 0)),
        scratch_shapes=[
            pltpu.VMEM((ATTN_BLOCK, KV_WIDTH), F32),
            pltpu.VMEM((ATTN_BLOCK, KV_WIDTH), F32),
            pltpu.VMEM((tq, ATTN_IN), F32),
            pltpu.VMEM((tq, ATTN_WIDTH), F32),
            pltpu.VMEM((tq, ATTN_WIDTH), BF16),
        ],
        compiler_params=pltpu.CompilerParams(
            dimension_semantics=("arbitrary",),
            vmem_limit_bytes=VMEM_LIMIT_BYTES),
        name="swa_layer",
    )(sinks.astype(F32), x, positions.reshape(B, 1, T), x, inv_freq.reshape(ROPE_HALF, 1), _rope_spread_matrix(),
      pre_w.reshape(1, D), post_w.reshape(1, D),
      w_in.astype(BF16), b_in.reshape(1, ATTN_IN), w_out.astype(BF16), b_out.reshape(1, D))


def _boundary_rows(b, m):
    c, n = b.shape
    if 2 * m >= SUBLANES:
        b3 = b.reshape(c // (2 * m), 2 * m, n)
        return jnp.broadcast_to(b3[:, m - 1:m, :], b3.shape).reshape(c, n)
    b3 = b.reshape(c // SUBLANES, SUBLANES, n)
    sub = lax.broadcasted_iota(jnp.int32, b3.shape, 1)
    r = jnp.broadcast_to(b3[:, SUBLANES - m - 1:SUBLANES - m, :], b3.shape)
    for start in range(SUBLANES - 4 * m, -1, -2 * m):
        r = jnp.where(sub < start + 2 * m, jnp.broadcast_to(b3[:, start + m - 1:start + m, :], b3.shape), r)
    return r.reshape(c, n)


def _rec_layer_kernel(layer, chunks_per_seq, n_chunks, x_ref, xprev_ref, lbl_ref, lv_ref, ltri_ref, bsum_ref,
                      prew_ref, postw_ref, win_ref, gnw_ref, wout_ref, o_ref, state_ref, proj_ref, q_ref, k_ref,
                      ghi_ref, glo_ref, b_ref, gated_ref, y_ref):
    s = pl.program_id(0)
    c = jnp.minimum(s, n_chunks - 1) % chunks_per_seq
    chunk = x_ref.shape[0]
    seg = lv_ref.shape[0]
    seg_levels = seg.bit_length() - 1
    base_levels = REC_BASE.bit_length() - 1

    @pl.when(s == 0)
    def _():
        y_ref[...] = jnp.zeros_like(y_ref)

    @pl.when(c == 0)
    def _():
        state_ref[...] = jnp.zeros_like(state_ref)

    x = x_ref[...]
    h = _rmsnorm(x, prew_ref[...]).astype(BF16)

    def project(c0, c1):
        proj_ref[:, c0:c1] = _dot(h, win_ref[:, c0:c1])

    project(FORGET_DIM, 2 * FORGET_DIM)
    project(0, FORGET_DIM)
    project(2 * FORGET_DIM, 2 * FORGET_DIM + REC_WIDTH)

    def forget_gates(hd):
        off = hd * LANES
        f_raw = proj_ref[:, FORGET_DIM + off:FORGET_DIM + off + LANES]

        logits = lbl_ref[:, off:off + LANES]
        ex = jnp.exp(logits - jnp.max(logits, axis=0, keepdims=True))
        lb = jnp.sum(ex[1:layer + 1], axis=0, keepdims=True) / jnp.sum(ex, axis=0, keepdims=True)

        e = jnp.exp(-jnp.abs(f_raw))
        sp = 1.0 / (1.0 + e)
        pos = f_raw >= 0
        sig = jnp.where(pos, sp, e * sp)
        nsig = jnp.where(pos, e * sp, sp)
        log2_f = jnp.log2(lb + (1.0 - lb) * sig)
        k_ref[:, off:off + LANES] = ((1.0 - lb) * nsig).astype(BF16)
        g_hi = log2_f.astype(BF16)
        ghi_ref[:, off:off + LANES] = g_hi
        glo_ref[:, off:off + LANES] = (log2_f - g_hi.astype(F32)).astype(BF16)

    for hd in range(REC_HEADS):
        forget_gates(hd)

    o_ref[...] = xprev_ref[...] + _rmsnorm(y_ref[...], postw_ref[...])

    bsum = bsum_ref[...]
    spread = -jnp.min(_dot(bsum, ghi_ref[...]) + _dot(bsum, glo_ref[...]))

    ltri = ltri_ref[...]
    b_all = _dot(ltri, ghi_ref[...]) + _dot(ltri, glo_ref[...])
    for hd in range(REC_HEADS):
        b_ref[hd] = b_all[:, hd * LANES:(hd + 1) * LANES]
        q_raw = proj_ref[:, hd * LANES:(hd + 1) * LANES]
        q_ref[:, hd * LANES:(hd + 1) * LANES] = (q_raw * _sigmoid(q_raw)).astype(BF16)

    def head_scores(hd, bounded):
        off = hd * LANES
        b = b_ref[hd]
        q = q_ref[:, off:off + LANES]
        k = k_ref[:, off:off + LANES]
        lv = lv_ref[...]

        seg_scores = []
        for s0 in range(0, chunk, seg):
            rows = slice(s0, s0 + seg)
            qs, ks, bs = q[rows], k[rows], b[rows]
            if bounded:
                b3 = bs.reshape(seg // REC_BASE, REC_BASE, LANES)
                mid = 0.5 * (b3[:, 0:1, :] + b3[:, REC_BASE - 1:REC_BASE, :])
                r = jnp.broadcast_to(mid, b3.shape).reshape(seg, LANES)
                a = jnp.where(lv < base_levels,
                              _dot(qs * jnp.exp2(bs - r).astype(BF16),
                                   jnp.transpose(ks.astype(F32) * jnp.exp2(r - bs)).astype(BF16)), 0.0)
                first_level = base_levels
            else:
                a = jnp.where(lv == -1, _dot_nt(qs, ks), 0.0)
                first_level = 0
            for j in range(first_level, seg_levels):
                en = jnp.exp2(-jnp.abs(bs - _boundary_rows(bs, 1 << j))).astype(BF16)
                a = jnp.where(lv == j, _dot_nt(qs * en, ks * en), a)
            seg_scores.append(a.astype(BF16))
        block_scores = []
        m = seg
        while m < chunk:
            for base in range(0, chunk, 2 * m):
                lo, hi = slice(base, base + m), slice(base + m, base + 2 * m)
                r = b[base + m - 1:base + m]
                qh = q[hi] * jnp.exp2(b[hi] - r).astype(BF16)
                kh_t = jnp.transpose(k[lo].astype(F32) * jnp.exp2(r - b[lo])).astype(BF16)
                block_scores.append((base, m, _dot(qh, kh_t).astype(BF16)))
            m *= 2
        b_last = b[chunk - 1:chunk, :]
        st = state_ref[hd]
        o_inter = _dot(q * jnp.exp2(b).astype(BF16), jnp.transpose(st).astype(BF16))
        kd = k * jnp.exp2(b_last - b).astype(BF16)
        return seg_scores, block_scores, o_inter, kd, st * jnp.exp2(b_last)

    def head_finish(hd, scores):
        seg_scores, block_scores, o_inter, kd, st_decayed = scores
        off = 2 * FORGET_DIM + hd * LANES
        v = proj_ref[:, off:off + LANES].astype(BF16)
        outs = [_dot(a, v[i * seg:(i + 1) * seg]) for i, a in enumerate(seg_scores)]
        for base, m, a in block_scores:
            upd = _dot(a, v[base:base + m])
            for i in range(m // seg):
                outs[(base + m) // seg + i] += upd[i * seg:(i + 1) * seg]
        o = jnp.concatenate(outs, axis=0) + o_inter
        state_ref[hd] = st_decayed + _dot_tn(v, kd)
        return o * lax.rsqrt(jnp.mean(o * o, axis=-1, keepdims=True) + NORM_EPS) * gnw_ref[...]

    def mix_and_output(bounded):
        z0 = 2 * FORGET_DIM + REC_WIDTH
        z_chunks = [(z0 + i * REC_WIDTH // 4, z0 + (i + 1) * REC_WIDTH // 4) for i in range(4)]
        heads_per_chunk = REC_HEADS // len(z_chunks)
        pending = []

        def gate(upto_chunk):
            while pending and pending[0][0] // heads_per_chunk < upto_chunk:
                hd, o = pending.pop(0)
                z = proj_ref[:, z0 + hd * LANES:z0 + (hd + 1) * LANES]
                gated_ref[:, hd * LANES:(hd + 1) * LANES] = (o * (z * _sigmoid(z))).astype(BF16)

        n_gate_chunks = len(z_chunks)
        ahead = 2
        queue = [head_scores(i, bounded) for i in range(ahead)]
        for hd in range(REC_HEADS):
            scores = queue.pop(0)
            if hd + ahead < REC_HEADS:
                queue.append(head_scores(hd + ahead, bounded))
            pending.append((hd, head_finish(hd, scores)))
            gate(min(hd, n_gate_chunks))
            if z_chunks:
                project(*z_chunks.pop(0))
        gate(n_gate_chunks)
        y_ref[...] = _dot(gated_ref[...], wout_ref[...])

    lax.cond(0.5 * spread <= SAFE_LOG2_EXPONENT, lambda: mix_and_output(True), lambda: mix_and_output(False))


def _level_index(c):
    t = np.arange(c)[:, None]
    s = np.arange(c)[None, :]
    lv = np.floor(np.log2(np.maximum(t ^ s, 1))).astype(np.int32)
    lv = np.where(s < t, lv, np.where(s == t, -1, 99))
    return jnp.asarray(lv, dtype=jnp.int32)


def _rec_layer(x, layer, lb_logits, pre_w, post_w, w_in, gnorm_w, w_out):
    B, T, D = x.shape
    chunk = min(REC_CHUNK, T)
    seg = min(REC_SEG, chunk)
    assert T % chunk == 0 and chunk & (chunk - 1) == 0 and seg >= 2 * SUBLANES and D == D_MODEL
    depth = lb_logits.shape[0]
    ltri = jnp.asarray(np.tril(np.ones((chunk, chunk), np.float32)), dtype=BF16)
    bsum_np = np.zeros((max(SUBLANES, chunk // REC_BASE), chunk), np.float32)
    for i in range(chunk // REC_BASE):
        bsum_np[i, i * REC_BASE + 1:(i + 1) * REC_BASE] = 1.0
    bsum = jnp.asarray(bsum_np, dtype=BF16)
    const = lambda s: (0, 0)
    n_chunks, cur, prev = _staggered_tiles(B, T // chunk)
    return pl.pallas_call(
        functools.partial(_rec_layer_kernel, layer, T // chunk, n_chunks),
        out_shape=jax.ShapeDtypeStruct((B, T, D), x.dtype),
        grid=(n_chunks + 1,),
        in_specs=[
            pl.BlockSpec((None, chunk, D), lambda s: (*cur(s), 0)),
            pl.BlockSpec((None, chunk, D), lambda s: (*prev(s), 0)),
            pl.BlockSpec((depth, FORGET_DIM), const),
            pl.BlockSpec((seg, seg), const),
            pl.BlockSpec((chunk, chunk), const),
            pl.BlockSpec(bsum.shape, const),
            pl.BlockSpec((1, D), const),
            pl.BlockSpec((1, D), const),
            pl.BlockSpec((D, REC_IN), const),
            pl.BlockSpec((1, REC_VALUE_DIM), const),
            pl.BlockSpec((REC_WIDTH, D), const),
        ],
        out_specs=pl.BlockSpec((None, chunk, D), lambda s: (*prev(s), 0)),
        scratch_shapes=[
            pltpu.VMEM((REC_HEADS, REC_VALUE_DIM, REC_KEY_DIM), F32),
            pltpu.VMEM((chunk, REC_IN), F32),
            pltpu.VMEM((chunk, FORGET_DIM), BF16),
            pltpu.VMEM((chunk, FORGET_DIM), BF16),
            pltpu.VMEM((chunk, FORGET_DIM), BF16),
            pltpu.VMEM((chunk, FORGET_DIM), BF16),
            pltpu.VMEM((REC_HEADS, chunk, REC_KEY_DIM), F32),
            pltpu.VMEM((chunk, REC_WIDTH), BF16),
            pltpu.VMEM((chunk, D), F32),
        ],
        compiler_params=pltpu.CompilerParams(
            dimension_semantics=("arbitrary",),
            vmem_limit_bytes=VMEM_LIMIT_BYTES),
        name="hgrn2_layer",
    )(x, x, lb_logits.astype(F32), _level_index(seg), ltri, bsum, pre_w.reshape(1, D), post_w.reshape(1, D),
      w_in.astype(BF16), gnorm_w.reshape(1, REC_VALUE_DIM), w_out.astype(BF16))


def kernel(x, positions, pre_norm_w, post_norm_w, attn_w_in, attn_b_in, attn_sinks, attn_w_out, attn_b_out,
           rec_w_in, rec_lb_logits, rec_gnorm_w, rec_w_out):
    depth = pre_norm_w.shape[0]
    for layer in range(depth):
        j = layer // N_MIXERS
        if layer % N_MIXERS == 0:
            x = _attn_layer(x, positions, pre_norm_w[layer], post_norm_w[layer], attn_w_in[j], attn_b_in[j],
                            attn_sinks[j], attn_w_out[j], attn_b_out[j])
        else:
            x = _rec_layer(x, layer, rec_lb_logits, pre_norm_w[layer], post_norm_w[layer], rec_w_in[j],
                           rec_gnorm_w[j], rec_w_out[j])
    return x
```

```python
import functools

import numpy as np
import jax
import jax.numpy as jnp
from jax import lax
from jax.experimental import pallas as pl
from jax.experimental.pallas import tpu as pltpu

D_MODEL = 1024
N_MIXERS = 2

HEAD_DIM = 64
N_HEADS = D_MODEL // HEAD_DIM
N_KV_HEADS = 2
GROUP = N_HEADS // N_KV_HEADS
ATTN_WIDTH = N_HEADS * HEAD_DIM
KV_WIDTH = N_KV_HEADS * HEAD_DIM
ATTN_IN = 2 * ATTN_WIDTH + 2 * KV_WIDTH
ATTN_BLOCK = 128
ROPE_THETA = 500000.0
ROPE_DIM = HEAD_DIM // 4
ROPE_HALF = ROPE_DIM // 2

REC_HEADS = 8
REC_KEY_DIM = 128
REC_VALUE_DIM = D_MODEL // REC_HEADS
FORGET_DIM = REC_HEADS * REC_KEY_DIM
REC_WIDTH = REC_HEADS * REC_VALUE_DIM
REC_IN = 2 * FORGET_DIM + 2 * REC_WIDTH

NORM_EPS = 1e-6
LOG2_E = 1.4426950408889634

LANES = 128
SUBLANES = 8
ATTN_TILE = 1024
REC_CHUNK = 256
REC_SEG = 128
REC_BASE = 128
SAFE_LOG2_EXPONENT = 112.0
VMEM_LIMIT_BYTES = 48 * 1024 * 1024

F32 = jnp.float32
BF16 = jnp.bfloat16


def _staggered_tiles(batch, tiles_per_seq):
    n = batch * tiles_per_seq

    def cur(s):
        c = jnp.minimum(s, n - 1)
        return c // tiles_per_seq, c % tiles_per_seq

    def prev(s):
        p = jnp.maximum(s - 1, 0)
        return p // tiles_per_seq, p % tiles_per_seq

    return n, cur, prev


def _rmsnorm(x, w):
    return x * lax.rsqrt(jnp.mean(x * x, axis=-1, keepdims=True) + NORM_EPS) * w


def _sigmoid(x):
    return 1.0 / (1.0 + jnp.exp(-x))


def _dot(a, b):
    return jnp.dot(a, b, preferred_element_type=F32)


def _dot_nt(a, b):
    return lax.dot_general(a, b, (((1,), (1,)), ((), ())), preferred_element_type=F32)


def _dot_tn(a, b):
    return lax.dot_general(a, b, (((0,), (0,)), ((), ())), preferred_element_type=F32)


def _split_dot_tn(x, p):
    acc = None
    for _ in range(3):
        part = x.astype(BF16)
        x = x - part.astype(F32)
        term = _dot_tn(part, p)
        acc = term if acc is None else acc + term
    return acc


def _attn_layer_kernel(tiles_per_seq, n_tiles, sinks_ref, x_ref, pos_ref, xprev_ref, invf_ref, spread_ref,
                       prew_ref, postw_ref, win_ref, bin_ref, wout_ref, bout_ref, o_ref,
                       kprev_ref, vprev_ref, proj_ref, att_ref, gated_ref):
    s = pl.program_id(0)
    t = jnp.minimum(s, n_tiles - 1) % tiles_per_seq
    tq = x_ref.shape[0]
    n_blocks = tq // ATTN_BLOCK

    @pl.when(s == 0)
    def _():
        gated_ref[...] = jnp.zeros_like(gated_ref)

    @pl.when(t == 0)
    def _():
        kprev_ref[...] = jnp.zeros_like(kprev_ref)
        vprev_ref[...] = jnp.zeros_like(vprev_ref)

    y_prev = _dot(gated_ref[...], wout_ref[...]) + bout_ref[...]

    ang_t = invf_ref[...] * pos_ref[...].astype(F32)
    trig_t = jnp.concatenate([jnp.cos(ang_t), jnp.sin(ang_t)], axis=0)
    tabs = _split_dot_tn(trig_t, spread_ref[...])
    lane = lax.broadcasted_iota(jnp.int32, (1, LANES), 1)
    d = lane % HEAD_DIM
    c_tab = tabs[:, :LANES] + jnp.where(d < ROPE_DIM, 0.0, 1.0)
    s_lo = tabs[:, LANES:2 * LANES]
    s_hi = tabs[:, 2 * LANES:]

    x = x_ref[...]
    h = _rmsnorm(x, prew_ref[...]).astype(BF16)

    def project(c0, c1):
        proj_ref[:, c0:c1] = _dot(h, win_ref[:, c0:c1]) + bin_ref[:, c0:c1]

    q_cols = GROUP * HEAD_DIM
    z0 = ATTN_WIDTH + 2 * KV_WIDTH
    project(ATTN_WIDTH, z0)
    project(0, q_cols)
    project(q_cols, 2 * q_cols)
    later_chunks = [(g * q_cols, (g + 1) * q_cols) for g in range(2, N_KV_HEADS)]
    later_chunks += [(z0 + i * ATTN_WIDTH // 4, z0 + (i + 1) * ATTN_WIDTH // 4) for i in range(4)]

    o_ref[...] = xprev_ref[...] + _rmsnorm(y_prev, postw_ref[...])

    def rope(xc, rows=slice(None)):
        return (xc * c_tab[rows] + pltpu.roll(xc, LANES - ROPE_HALF, 1) * s_lo[rows]
                + pltpu.roll(xc, ROPE_HALF, 1) * s_hi[rows])

    k_all = rope(proj_ref[:, ATTN_WIDTH:ATTN_WIDTH + KV_WIDTH])
    v_all = proj_ref[:, ATTN_WIDTH + KV_WIDTH:ATTN_WIDTH + 2 * KV_WIDTH]

    lo_half = lane < HEAD_DIM
    from_prev = (lax.broadcasted_iota(jnp.int32, (ATTN_BLOCK, ATTN_BLOCK), 1)
                 > lax.broadcasted_iota(jnp.int32, (ATTN_BLOCK, ATTN_BLOCK), 0))
    no_prev = jnp.where(t == 0, -jnp.inf, 0.0)
    scale = HEAD_DIM ** -0.5 * LOG2_E

    def block_diag(a):
        a_sw = pltpu.roll(a, HEAD_DIM, 1)
        zero = jnp.zeros_like(a)
        g0 = jnp.concatenate([jnp.where(lo_half, a, zero), jnp.where(lo_half, zero, a_sw)], axis=0)
        g1 = jnp.concatenate([jnp.where(lo_half, a_sw, zero), jnp.where(lo_half, zero, a)], axis=0)
        return g0.astype(BF16), g1.astype(BF16)

    k_prev = kprev_ref[...]
    v_prev = vprev_ref[...]
    kbs, vbs = [], []
    for j in range(n_blocks):
        rows = slice(j * ATTN_BLOCK, (j + 1) * ATTN_BLOCK)
        k_cur, v_cur = k_all[rows], v_all[rows]
        kbs.append(block_diag(jnp.concatenate([k_prev, k_cur], axis=0)))
        vbs.append(block_diag(jnp.concatenate([v_prev, v_cur], axis=0)))
        k_prev, v_prev = k_cur, v_cur
    kprev_ref[...] = k_prev
    vprev_ref[...] = v_prev

    pairs_per_group = GROUP // 2

    def scores(j, g):
        rows = slice(j * ATTN_BLOCK, (j + 1) * ATTN_BLOCK)
        qs = [(rope(proj_ref[rows, p * LANES:(p + 1) * LANES], rows) * scale).astype(BF16)
              for p in range(g * pairs_per_group, (g + 1) * pairs_per_group)]
        return _dot_nt(jnp.concatenate(qs, axis=0), kbs[j][g])

    items = [(j, g) for g in range(N_KV_HEADS) for j in range(n_blocks)]
    s_next = scores(*items[0])
    for i, (j, g) in enumerate(items):
        s_group = s_next
        if later_chunks and i % 2 == 0:
            project(*later_chunks.pop(0))
        if i + 1 < len(items):
            s_next = scores(*items[i + 1])
        probs, invs = [], []
        for pi in range(pairs_per_group):
            p = g * pairs_per_group + pi
            s = s_group[pi * ATTN_BLOCK:(pi + 1) * ATTN_BLOCK]
            halves, inv = [], []
            for e in range(2):
                s_prev = s[:, 2 * e * ATTN_BLOCK:(2 * e + 1) * ATTN_BLOCK]
                s_cur = s[:, (2 * e + 1) * ATTN_BLOCK:(2 * e + 2) * ATTN_BLOCK]
                if j == 0:
                    s_prev = s_prev + no_prev
                sw = jnp.where(from_prev, s_prev, s_cur)
                sink = sinks_ref[2 * p + e] * LOG2_E
                m = jnp.maximum(jnp.max(sw, axis=-1, keepdims=True), sink)
                pw = jnp.exp2(sw - m)
                denom = jnp.sum(pw, axis=-1, keepdims=True) + jnp.exp2(sink - m)
                pw = pw.astype(BF16)
                zero = jnp.zeros_like(pw)
                halves += [jnp.where(from_prev, pw, zero), jnp.where(from_prev, zero, pw)]
                inv.append(1.0 / denom)
            probs.append(jnp.concatenate(halves, axis=1))
            invs.append(jnp.where(lo_half, inv[0], inv[1]))
        o = _dot(jnp.concatenate(probs, axis=0), vbs[j][g])
        for pi in range(pairs_per_group):
            p = g * pairs_per_group + pi
            att_ref[j * ATTN_BLOCK:(j + 1) * ATTN_BLOCK, p * LANES:(p + 1) * LANES] = (
                o[pi * ATTN_BLOCK:(pi + 1) * ATTN_BLOCK] * invs[pi])
    for chunk in later_chunks:
        project(*chunk)

    z = proj_ref[:, z0:]
    gated_ref[...] = (att_ref[...] * (z * _sigmoid(z))).astype(BF16)


def _rope_spread_matrix():
    p = np.zeros((2 * ROPE_HALF, 3 * LANES), np.float32)
    for lane in range(LANES):
        d = lane % HEAD_DIM
        if d < ROPE_DIM:
            p[d % ROPE_HALF, lane] = 1.0
        if d < ROPE_HALF:
            p[ROPE_HALF + d, LANES + lane] = -1.0
        elif d < ROPE_DIM:
            p[ROPE_HALF + d - ROPE_HALF, 2 * LANES + lane] = 1.0
    return jnp.asarray(p, dtype=BF16)


def _attn_layer(x, positions, pre_w, post_w, w_in, b_in, sinks, w_out, b_out):
    B, T, D = x.shape
    tq = min(ATTN_TILE, T)
    assert T % tq == 0 and tq % ATTN_BLOCK == 0 and D == D_MODEL
    inv_freq = ROPE_THETA ** (-(jnp.arange(ROPE_HALF, dtype=F32) * 2.0 / ROPE_DIM))
    const = lambda s: (0, 0)
    nt = T // tq
    n_tiles, cur, prev = _staggered_tiles(B, nt)
    return pl.pallas_call(
        functools.partial(_attn_layer_kernel, nt, n_tiles),
        out_shape=jax.ShapeDtypeStruct((B, T, D), x.dtype),
        grid=(n_tiles + 1,),
        in_specs=[
            pl.BlockSpec(memory_space=pltpu.SMEM),
            pl.BlockSpec((None, tq, D), lambda s: (*cur(s), 0)),
            pl.BlockSpec((None, 1, tq), lambda s: (cur(s)[0], 0, cur(s)[1])),
            pl.BlockSpec((None, tq, D), lambda s: (*prev(s), 0)),
            pl.BlockSpec((ROPE_HALF, 1), const),
            pl.BlockSpec((2 * ROPE_HALF, 3 * LANES), const),
            pl.BlockSpec((1, D), const),
            pl.BlockSpec((1, D), const),
            pl.BlockSpec((D, ATTN_IN), const),
            pl.BlockSpec((1, ATTN_IN), const),
            pl.BlockSpec((ATTN_WIDTH, D), const),
            pl.BlockSpec((1, D), const),
        ],
        out_specs=pl.BlockSpec((None, tq, D), lambda s: (*prev(s), 0)),
        scratch_shapes=[
            pltpu.VMEM((ATTN_BLOCK, KV_WIDTH), F32),
            pltpu.VMEM((ATTN_BLOCK, KV_WIDTH), F32),
            pltpu.VMEM((tq, ATTN_IN), F32),
            pltpu.VMEM((tq, ATTN_WIDTH), F32),
            pltpu.VMEM((tq, ATTN_WIDTH), BF16),
        ],
        compiler_params=pltpu.CompilerParams(
            dimension_semantics=("arbitrary",),
            vmem_limit_bytes=VMEM_LIMIT_BYTES),
        name="swa_layer",
    )(sinks.astype(F32), x, positions.reshape(B, 1, T), x, inv_freq.reshape(ROPE_HALF, 1), _rope_spread_matrix(),
      pre_w.reshape(1, D), post_w.reshape(1, D),
      w_in.astype(BF16), b_in.reshape(1, ATTN_IN), w_out.astype(BF16), b_out.reshape(1, D))


def _boundary_rows(b, m):
    c, n = b.shape
    if 2 * m >= SUBLANES:
        b3 = b.reshape(c // (2 * m), 2 * m, n)
        return jnp.broadcast_to(b3[:, m - 1:m, :], b3.shape).reshape(c, n)
    b3 = b.reshape(c // SUBLANES, SUBLANES, n)
    sub = lax.broadcasted_iota(jnp.int32, b3.shape, 1)
    r = jnp.broadcast_to(b3[:, SUBLANES - m - 1:SUBLANES - m, :], b3.shape)
    for start in range(SUBLANES - 4 * m, -1, -2 * m):
        r = jnp.where(sub < start + 2 * m, jnp.broadcast_to(b3[:, start + m - 1:start + m, :], b3.shape), r)
    return r.reshape(c, n)


def _rec_layer_kernel(layer, chunks_per_seq, n_chunks, x_ref, xprev_ref, lbl_ref, lv_ref, ltri_ref, bsum_ref,
                      prew_ref, postw_ref, win_ref, gnw_ref, wout_ref, o_ref, state_ref, proj_ref, q_ref, k_ref,
                      ghi_ref, glo_ref, b_ref, gated_ref, y_ref):
    s = pl.program_id(0)
    c = jnp.minimum(s, n_chunks - 1) % chunks_per_seq
    chunk = x_ref.shape[0]
    seg = lv_ref.shape[0]
    seg_levels = seg.bit_length() - 1
    base_levels = REC_BASE.bit_length() - 1

    @pl.when(s == 0)
    def _():
        y_ref[...] = jnp.zeros_like(y_ref)

    @pl.when(c == 0)
    def _():
        state_ref[...] = jnp.zeros_like(state_ref)

    x = x_ref[...]
    h = _rmsnorm(x, prew_ref[...]).astype(BF16)

    def project(c0, c1):
        proj_ref[:, c0:c1] = _dot(h, win_ref[:, c0:c1])

    project(FORGET_DIM, 2 * FORGET_DIM)
    project(0, FORGET_DIM)
    project(2 * FORGET_DIM, 2 * FORGET_DIM + REC_WIDTH)

    def forget_gates(hd):
        off = hd * LANES
        f_raw = proj_ref[:, FORGET_DIM + off:FORGET_DIM + off + LANES]

        logits = lbl_ref[:, off:off + LANES]
        ex = jnp.exp(logits - jnp.max(logits, axis=0, keepdims=True))
        lb = jnp.sum(ex[1:layer + 1], axis=0, keepdims=True) / jnp.sum(ex, axis=0, keepdims=True)

        e = jnp.exp(-jnp.abs(f_raw))
        sp = 1.0 / (1.0 + e)
        pos = f_raw >= 0
        sig = jnp.where(pos, sp, e * sp)
        nsig = jnp.where(pos, e * sp, sp)
        log2_f = jnp.log2(lb + (1.0 - lb) * sig)
        k_ref[:, off:off + LANES] = ((1.0 - lb) * nsig).astype(BF16)
        g_hi = log2_f.astype(BF16)
        ghi_ref[:, off:off + LANES] = g_hi
        glo_ref[:, off:off + LANES] = (log2_f - g_hi.astype(F32)).astype(BF16)

    for hd in range(REC_HEADS):
        forget_gates(hd)

    o_ref[...] = xprev_ref[...] + _rmsnorm(y_ref[...], postw_ref[...])

    bsum = bsum_ref[...]
    spread = -jnp.min(_dot(bsum, ghi_ref[...]) + _dot(bsum, glo_ref[...]))

    ltri = ltri_ref[...]
    b_all = _dot(ltri, ghi_ref[...]) + _dot(ltri, glo_ref[...])
    for hd in range(REC_HEADS):
        b_ref[hd] = b_all[:, hd * LANES:(hd + 1) * LANES]
        q_raw = proj_ref[:, hd * LANES:(hd + 1) * LANES]
        q_ref[:, hd * LANES:(hd + 1) * LANES] = (q_raw * _sigmoid(q_raw)).astype(BF16)

    def head_scores(hd, bounded):
        off = hd * LANES
        b = b_ref[hd]
        q = q_ref[:, off:off + LANES]
        k = k_ref[:, off:off + LANES]
        lv = lv_ref[...]

        seg_scores = []
        for s0 in range(0, chunk, seg):
            rows = slice(s0, s0 + seg)
            qs, ks, bs = q[rows], k[rows], b[rows]
            if bounded:
                b3 = bs.reshape(seg // REC_BASE, REC_BASE, LANES)
                mid = 0.5 * (b3[:, 0:1, :] + b3[:, REC_BASE - 1:REC_BASE, :])
                r = jnp.broadcast_to(mid, b3.shape).reshape(seg, LANES)
                a = jnp.where(lv < base_levels,
                              _dot(qs * jnp.exp2(bs - r).astype(BF16),
                                   jnp.transpose(ks.astype(F32) * jnp.exp2(r - bs)).astype(BF16)), 0.0)
                first_level = base_levels
            else:
                a = jnp.where(lv == -1, _dot_nt(qs, ks), 0.0)
                first_level = 0
            for j in range(first_level, seg_levels):
                en = jnp.exp2(-jnp.abs(bs - _boundary_rows(bs, 1 << j))).astype(BF16)
                a = jnp.where(lv == j, _dot_nt(qs * en, ks * en), a)
            seg_scores.append(a.astype(BF16))
        block_scores = []
        m = seg
        while m < chunk:
            for base in range(0, chunk, 2 * m):
                lo, hi = slice(base, base + m), slice(base + m, base + 2 * m)
                r = b[base + m - 1:base + m]
                qh = q[hi] * jnp.exp2(b[hi] - r).astype(BF16)
                kh_t = jnp.transpose(k[lo].astype(F32) * jnp.exp2(r - b[lo])).astype(BF16)
                block_scores.append((base, m, _dot(qh, kh_t).astype(BF16)))
            m *= 2
        b_last = b[chunk - 1:chunk, :]
        st = state_ref[hd]
        o_inter = _dot(q * jnp.exp2(b).astype(BF16), jnp.transpose(st).astype(BF16))
        kd = k * jnp.exp2(b_last - b).astype(BF16)
        return seg_scores, block_scores, o_inter, kd, st * jnp.exp2(b_last)

    def head_finish(hd, scores):
        seg_scores, block_scores, o_inter, kd, st_decayed = scores
        off = 2 * FORGET_DIM + hd * LANES
        v = proj_ref[:, off:off + LANES].astype(BF16)
        outs = [_dot(a, v[i * seg:(i + 1) * seg]) for i, a in enumerate(seg_scores)]
        for base, m, a in block_scores:
            upd = _dot(a, v[base:base + m])
            for i in range(m // seg):
                outs[(base + m) // seg + i] += upd[i * seg:(i + 1) * seg]
        o = jnp.concatenate(outs, axis=0) + o_inter
        state_ref[hd] = st_decayed + _dot_tn(v, kd)
        return o * lax.rsqrt(jnp.mean(o * o, axis=-1, keepdims=True) + NORM_EPS) * gnw_ref[...]

    def mix_and_output(bounded):
        z0 = 2 * FORGET_DIM + REC_WIDTH
        z_chunks = [(z0 + i * REC_WIDTH // 4, z0 + (i + 1) * REC_WIDTH // 4) for i in range(4)]
        heads_per_chunk = REC_HEADS // len(z_chunks)
        pending = []

        def gate(upto_chunk):
            while pending and pending[0][0] // heads_per_chunk < upto_chunk:
                hd, o = pending.pop(0)
                z = proj_ref[:, z0 + hd * LANES:z0 + (hd + 1) * LANES]
                gated_ref[:, hd * LANES:(hd + 1) * LANES] = (o * (z * _sigmoid(z))).astype(BF16)

        n_gate_chunks = len(z_chunks)
        ahead = 2
        queue = [head_scores(i, bounded) for i in range(ahead)]
        for hd in range(REC_HEADS):
            scores = queue.pop(0)
            if hd + ahead < REC_HEADS:
                queue.append(head_scores(hd + ahead, bounded))
            pending.append((hd, head_finish(hd, scores)))
            gate(min(hd, n_gate_chunks))
            if z_chunks:
                project(*z_chunks.pop(0))
        gate(n_gate_chunks)
        y_ref[...] = _dot(gated_ref[...], wout_ref[...])

    lax.cond(0.5 * spread <= SAFE_LOG2_EXPONENT, lambda: mix_and_output(True), lambda: mix_and_output(False))


def _level_index(c):
    t = np.arange(c)[:, None]
    s = np.arange(c)[None, :]
    lv = np.floor(np.log2(np.maximum(t ^ s, 1))).astype(np.int32)
    lv = np.where(s < t, lv, np.where(s == t, -1, 99))
    return jnp.asarray(lv, dtype=jnp.int32)


def _rec_layer(x, layer, lb_logits, pre_w, post_w, w_in, gnorm_w, w_out):
    B, T, D = x.shape
    chunk = min(REC_CHUNK, T)
    seg = min(REC_SEG, chunk)
    assert T % chunk == 0 and chunk & (chunk - 1) == 0 and seg >= 2 * SUBLANES and D == D_MODEL
    assert seg % REC_BASE == 0 and chunk % seg == 0
    depth = lb_logits.shape[0]
    ltri = jnp.asarray(np.tril(np.ones((chunk, chunk), np.float32)), dtype=BF16)
    bsum_np = np.zeros((max(SUBLANES, chunk // REC_BASE), chunk), np.float32)
    for i in range(chunk // REC_BASE):
        bsum_np[i, i * REC_BASE + 1:(i + 1) * REC_BASE] = 1.0
    bsum = jnp.asarray(bsum_np, dtype=BF16)
    const = lambda s: (0, 0)
    n_chunks, cur, prev = _staggered_tiles(B, T // chunk)
    return pl.pallas_call(
        functools.partial(_rec_layer_kernel, layer, T // chunk, n_chunks),
        out_shape=jax.ShapeDtypeStruct((B, T, D), x.dtype),
        grid=(n_chunks + 1,),
        in_specs=[
            pl.BlockSpec((None, chunk, D), lambda s: (*cur(s), 0)),
            pl.BlockSpec((None, chunk, D), lambda s: (*prev(s), 0)),
            pl.BlockSpec((depth, FORGET_DIM), const),
            pl.BlockSpec((seg, seg), const),
            pl.BlockSpec((chunk, chunk), const),
            pl.BlockSpec(bsum.shape, const),
            pl.BlockSpec((1, D), const),
            pl.BlockSpec((1, D), const),
            pl.BlockSpec((D, REC_IN), const),
            pl.BlockSpec((1, REC_VALUE_DIM), const),
            pl.BlockSpec((REC_WIDTH, D), const),
        ],
        out_specs=pl.BlockSpec((None, chunk, D), lambda s: (*prev(s), 0)),
        scratch_shapes=[
            pltpu.VMEM((REC_HEADS, REC_VALUE_DIM, REC_KEY_DIM), F32),
            pltpu.VMEM((chunk, REC_IN), F32),
            pltpu.VMEM((chunk, FORGET_DIM), BF16),
            pltpu.VMEM((chunk, FORGET_DIM), BF16),
            pltpu.VMEM((chunk, FORGET_DIM), BF16),
            pltpu.VMEM((chunk, FORGET_DIM), BF16),
            pltpu.VMEM((REC_HEADS, chunk, REC_KEY_DIM), F32),
            pltpu.VMEM((chunk, REC_WIDTH), BF16),
            pltpu.VMEM((chunk, D), F32),
        ],
        compiler_params=pltpu.CompilerParams(
            dimension_semantics=("arbitrary",),
            vmem_limit_bytes=VMEM_LIMIT_BYTES),
        name="hgrn2_layer",
    )(x, x, lb_logits.astype(F32), _level_index(seg), ltri, bsum, pre_w.reshape(1, D), post_w.reshape(1, D),
      w_in.astype(BF16), gnorm_w.reshape(1, REC_VALUE_DIM), w_out.astype(BF16))


def kernel(x, positions, pre_norm_w, post_norm_w, attn_w_in, attn_b_in, attn_sinks, attn_w_out, attn_b_out,
           rec_w_in, rec_lb_logits, rec_gnorm_w, rec_w_out):
    depth = pre_norm_w.shape[0]
    for layer in range(depth):
        j = layer // N_MIXERS
        if layer % N_MIXERS == 0:
            x = _attn_layer(x, positions, pre_norm_w[layer], post_norm_w[layer], attn_w_in[j], attn_b_in[j],
                            attn_sinks[j], attn_w_out[j], attn_b_out[j])
        else:
            x = _rec_layer(x, layer, rec_lb_logits, pre_norm_w[layer], post_norm_w[layer], rec_w_in[j],
                           rec_gnorm_w[j], rec_w_out[j])
    return x
```

```python
import functools

import numpy as np
import jax
import jax.numpy as jnp
from jax import lax
from jax.experimental import pallas as pl
from jax.experimental.pallas import tpu as pltpu

D_MODEL = 1024
N_MIXERS = 2

HEAD_DIM = 64
N_HEADS = D_MODEL // HEAD_DIM
N_KV_HEADS = 2
GROUP = N_HEADS // N_KV_HEADS
ATTN_WIDTH = N_HEADS * HEAD_DIM
KV_WIDTH = N_KV_HEADS * HEAD_DIM
ATTN_IN = 2 * ATTN_WIDTH + 2 * KV_WIDTH
ATTN_BLOCK = 128
ROPE_THETA = 500000.0
ROPE_DIM = HEAD_DIM // 4
ROPE_HALF = ROPE_DIM // 2

REC_HEADS = 8
REC_KEY_DIM = 128
REC_VALUE_DIM = D_MODEL // REC_HEADS
FORGET_DIM = REC_HEADS * REC_KEY_DIM
REC_WIDTH = REC_HEADS * REC_VALUE_DIM
REC_IN = 2 * FORGET_DIM + 2 * REC_WIDTH

NORM_EPS = 1e-6
LOG2_E = 1.4426950408889634

LANES = 128
SUBLANES = 8
ATTN_TILE = 512
REC_CHUNK = 256
REC_SEG = 128
REC_BASE = 128
SAFE_LOG2_EXPONENT = 112.0
VMEM_LIMIT_BYTES = 48 * 1024 * 1024

F32 = jnp.float32
BF16 = jnp.bfloat16


def _staggered_tiles(batch, tiles_per_seq):
    n = batch * tiles_per_seq

    def cur(s):
        c = jnp.minimum(s, n - 1)
        return c // tiles_per_seq, c % tiles_per_seq

    def prev(s):
        p = jnp.maximum(s - 1, 0)
        return p // tiles_per_seq, p % tiles_per_seq

    return n, cur, prev


def _rmsnorm(x, w):
    return x * lax.rsqrt(jnp.mean(x * x, axis=-1, keepdims=True) + NORM_EPS) * w


def _sigmoid(x):
    return 1.0 / (1.0 + jnp.exp(-x))


def _dot(a, b):
    return jnp.dot(a, b, preferred_element_type=F32)


def _dot_nt(a, b):
    return lax.dot_general(a, b, (((1,), (1,)), ((), ())), preferred_element_type=F32)


def _dot_tn(a, b):
    return lax.dot_general(a, b, (((0,), (0,)), ((), ())), preferred_element_type=F32)


def _split_dot_tn(x, p):
    acc = None
    for _ in range(3):
        part = x.astype(BF16)
        x = x - part.astype(F32)
        term = _dot_tn(part, p)
        acc = term if acc is None else acc + term
    return acc


def _attn_layer_kernel(tiles_per_seq, n_tiles, sinks_ref, x_ref, pos_ref, xprev_ref, invf_ref, spread_ref,
                       prew_ref, postw_ref, win_ref, bin_ref, wout_ref, bout_ref, o_ref,
                       kprev_ref, vprev_ref, proj_ref, att_ref, gated_ref):
    s = pl.program_id(0)
    t = jnp.minimum(s, n_tiles - 1) % tiles_per_seq
    tq = x_ref.shape[0]
    n_blocks = tq // ATTN_BLOCK

    @pl.when(s == 0)
    def _():
        gated_ref[...] = jnp.zeros_like(gated_ref)

    @pl.when(t == 0)
    def _():
        kprev_ref[...] = jnp.zeros_like(kprev_ref)
        vprev_ref[...] = jnp.zeros_like(vprev_ref)

    y_prev = _dot(gated_ref[...], wout_ref[...]) + bout_ref[...]

    ang_t = invf_ref[...] * pos_ref[...].astype(F32)
    trig_t = jnp.concatenate([jnp.cos(ang_t), jnp.sin(ang_t)], axis=0)
    tabs = _split_dot_tn(trig_t, spread_ref[...])
    lane = lax.broadcasted_iota(jnp.int32, (1, LANES), 1)
    d = lane % HEAD_DIM
    c_tab = tabs[:, :LANES] + jnp.where(d < ROPE_DIM, 0.0, 1.0)
    s_lo = tabs[:, LANES:2 * LANES]
    s_hi = tabs[:, 2 * LANES:]

    x = x_ref[...]
    h = _rmsnorm(x, prew_ref[...]).astype(BF16)

    def project(c0, c1):
        proj_ref[:, c0:c1] = _dot(h, win_ref[:, c0:c1]) + bin_ref[:, c0:c1]

    q_cols = GROUP * HEAD_DIM
    z0 = ATTN_WIDTH + 2 * KV_WIDTH
    project(ATTN_WIDTH, z0)
    project(0, q_cols)
    project(q_cols, 2 * q_cols)
    later_chunks = [(g * q_cols, (g + 1) * q_cols) for g in range(2, N_KV_HEADS)]
    later_chunks += [(z0 + i * ATTN_WIDTH // 4, z0 + (i + 1) * ATTN_WIDTH // 4) for i in range(4)]

    o_ref[...] = xprev_ref[...] + _rmsnorm(y_prev, postw_ref[...])

    def rope(xc, rows=slice(None)):
        return (xc * c_tab[rows] + pltpu.roll(xc, LANES - ROPE_HALF, 1) * s_lo[rows]
                + pltpu.roll(xc, ROPE_HALF, 1) * s_hi[rows])

    k_all = rope(proj_ref[:, ATTN_WIDTH:ATTN_WIDTH + KV_WIDTH])
    v_all = proj_ref[:, ATTN_WIDTH + KV_WIDTH:ATTN_WIDTH + 2 * KV_WIDTH]

    lo_half = lane < HEAD_DIM
    from_prev = (lax.broadcasted_iota(jnp.int32, (ATTN_BLOCK, ATTN_BLOCK), 1)
                 > lax.broadcasted_iota(jnp.int32, (ATTN_BLOCK, ATTN_BLOCK), 0))
    no_prev = jnp.where(t == 0, -jnp.inf, 0.0)
    scale = HEAD_DIM ** -0.5 * LOG2_E

    def block_diag(a):
        a_sw = pltpu.roll(a, HEAD_DIM, 1)
        zero = jnp.zeros_like(a)
        g0 = jnp.concatenate([jnp.where(lo_half, a, zero), jnp.where(lo_half, zero, a_sw)], axis=0)
        g1 = jnp.concatenate([jnp.where(lo_half, a_sw, zero), jnp.where(lo_half, zero, a)], axis=0)
        return g0.astype(BF16), g1.astype(BF16)

    k_prev = kprev_ref[...]
    v_prev = vprev_ref[...]
    kbs, vbs = [], []
    for j in range(n_blocks):
        rows = slice(j * ATTN_BLOCK, (j + 1) * ATTN_BLOCK)
        k_cur, v_cur = k_all[rows], v_all[rows]
        kbs.append(block_diag(jnp.concatenate([k_prev, k_cur], axis=0)))
        vbs.append(block_diag(jnp.concatenate([v_prev, v_cur], axis=0)))
        k_prev, v_prev = k_cur, v_cur
    kprev_ref[...] = k_prev
    vprev_ref[...] = v_prev

    pairs_per_group = GROUP // 2

    def scores(j, g):
        rows = slice(j * ATTN_BLOCK, (j + 1) * ATTN_BLOCK)
        qs = [(rope(proj_ref[rows, p * LANES:(p + 1) * LANES], rows) * scale).astype(BF16)
              for p in range(g * pairs_per_group, (g + 1) * pairs_per_group)]
        return _dot_nt(jnp.concatenate(qs, axis=0), kbs[j][g])

    items = [(j, g) for g in range(N_KV_HEADS) for j in range(n_blocks)]
    s_next = scores(*items[0])
    for i, (j, g) in enumerate(items):
        s_group = s_next
        if later_chunks and i % 2 == 0:
            project(*later_chunks.pop(0))
        if i + 1 < len(items):
            s_next = scores(*items[i + 1])
        probs, invs = [], []
        for pi in range(pairs_per_group):
            p = g * pairs_per_group + pi
            s = s_group[pi * ATTN_BLOCK:(pi + 1) * ATTN_BLOCK]
            halves, inv = [], []
            for e in range(2):
                s_prev = s[:, 2 * e * ATTN_BLOCK:(2 * e + 1) * ATTN_BLOCK]
                s_cur = s[:, (2 * e + 1) * ATTN_BLOCK:(2 * e + 2) * ATTN_BLOCK]
                if j == 0:
                    s_prev = s_prev + no_prev
                sw = jnp.where(from_prev, s_prev, s_cur)
                sink = sinks_ref[2 * p + e] * LOG2_E
                m = jnp.maximum(jnp.max(sw, axis=-1, keepdims=True), sink)
                pw = jnp.exp2(sw - m)
                denom = jnp.sum(pw, axis=-1, keepdims=True) + jnp.exp2(sink - m)
                pw = pw.astype(BF16)
                zero = jnp.zeros_like(pw)
                halves += [jnp.where(from_prev, pw, zero), jnp.where(from_prev, zero, pw)]
                inv.append(1.0 / denom)
            probs.append(jnp.concatenate(halves, axis=1))
            invs.append(jnp.where(lo_half, inv[0], inv[1]))
        o = _dot(jnp.concatenate(probs, axis=0), vbs[j][g])
        for pi in range(pairs_per_group):
            p = g * pairs_per_group + pi
            att_ref[j * ATTN_BLOCK:(j + 1) * ATTN_BLOCK, p * LANES:(p + 1) * LANES] = (
                o[pi * ATTN_BLOCK:(pi + 1) * ATTN_BLOCK] * invs[pi])
    for chunk in later_chunks:
        project(*chunk)

    z = proj_ref[:, z0:]
    gated_ref[...] = (att_ref[...] * (z * _sigmoid(z))).astype(BF16)


def _rope_spread_matrix():
    p = np.zeros((2 * ROPE_HALF, 3 * LANES), np.float32)
    for lane in range(LANES):
        d = lane % HEAD_DIM
        if d < ROPE_DIM:
            p[d % ROPE_HALF, lane] = 1.0
        if d < ROPE_HALF:
            p[ROPE_HALF + d, LANES + lane] = -1.0
        elif d < ROPE_DIM:
            p[ROPE_HALF + d - ROPE_HALF, 2 * LANES + lane] = 1.0
    return jnp.asarray(p, dtype=BF16)


def _attn_layer(x, positions, pre_w, post_w, w_in, b_in, sinks, w_out, b_out):
    B, T, D = x.shape
    tq = min(ATTN_TILE, T)
    assert T % tq == 0 and tq % ATTN_BLOCK == 0 and D == D_MODEL
    inv_freq = ROPE_THETA ** (-(jnp.arange(ROPE_HALF, dtype=F32) * 2.0 / ROPE_DIM))
    const = lambda s: (0, 0)
    nt = T // tq
    n_tiles, cur, prev = _staggered_tiles(B, nt)
    return pl.pallas_call(
        functools.partial(_attn_layer_kernel, nt, n_tiles),
        out_shape=jax.ShapeDtypeStruct((B, T, D), x.dtype),
        grid=(n_tiles + 1,),
        in_specs=[
            pl.BlockSpec(memory_space=pltpu.SMEM),
            pl.BlockSpec((None, tq, D), lambda s: (*cur(s), 0)),
            pl.BlockSpec((None, 1, tq), lambda s: (cur(s)[0], 0, cur(s)[1])),
            pl.BlockSpec((None, tq, D), lambda s: (*prev(s), 0)),
            pl.BlockSpec((ROPE_HALF, 1), const),
            pl.BlockSpec((2 * ROPE_HALF, 3 * LANES), const),
            pl.BlockSpec((1, D), const),
            pl.BlockSpec((1, D), const),
            pl.BlockSpec((D, ATTN_IN), const),
            pl.BlockSpec((1, ATTN_IN), const),
            pl.BlockSpec((ATTN_WIDTH, D), const),
            pl.BlockSpec((1, D), const),
        ],
        out_specs=pl.BlockSpec((None, tq, D), lambda s: (*prev(s), 0)),
        scratch_shapes=[
            pltpu.VMEM((ATTN_BLOCK, KV_WIDTH), F32),
            pltpu.VMEM((ATTN_BLOCK, KV_WIDTH), F32),
            pltpu.VMEM((tq, ATTN_IN), F32),
            pltpu.VMEM((tq, ATTN_WIDTH), F32),
            pltpu.VMEM((tq, ATTN_WIDTH), BF16),
        ],
        compiler_params=pltpu.CompilerParams(
            dimension_semantics=("arbitrary",),
            vmem_limit_bytes=VMEM_LIMIT_BYTES),
        name="swa_layer",
    )(sinks.astype(F32), x, positions.reshape(B, 1, T), x, inv_freq.reshape(ROPE_HALF, 1), _rope_spread_matrix(),
      pre_w.reshape(1, D), post_w.reshape(1, D),
      w_in.astype(BF16), b_in.reshape(1, ATTN_IN), w_out.astype(BF16), b_out.reshape(1, D))


def _boundary_rows(b, m):
    c, n = b.shape
    if 2 * m >= SUBLANES:
        b3 = b.reshape(c // (2 * m), 2 * m, n)
        return jnp.broadcast_to(b3[:, m - 1:m, :], b3.shape).reshape(c, n)
    b3 = b.reshape(c // SUBLANES, SUBLANES, n)
    sub = lax.broadcasted_iota(jnp.int32, b3.shape, 1)
    r = jnp.broadcast_to(b3[:, SUBLANES - m - 1:SUBLANES - m, :], b3.shape)
    for start in range(SUBLANES - 4 * m, -1, -2 * m):
        r = jnp.where(sub < start + 2 * m, jnp.broadcast_to(b3[:, start + m - 1:start + m, :], b3.shape), r)
    return r.reshape(c, n)


def _rec_layer_kernel(layer, chunks_per_seq, n_chunks, x_ref, lbl_ref, lv_ref, ltri_ref, bsum_ref,
                      prew_ref, postw_ref, win_ref, gnw_ref, wout_ref, o_ref, state_ref, proj_ref, q_ref, k_ref,
                      ghi_ref, glo_ref, b_ref, gated_ref, y_ref, xkeep_ref):
    s = pl.program_id(0)
    c = jnp.minimum(s, n_chunks - 1) % chunks_per_seq
    chunk = x_ref.shape[0]
    seg = lv_ref.shape[0]
    seg_levels = seg.bit_length() - 1
    base_levels = REC_BASE.bit_length() - 1

    @pl.when(s == 0)
    def _():
        y_ref[...] = jnp.zeros_like(y_ref)
        xkeep_ref[...] = jnp.zeros_like(xkeep_ref)

    @pl.when(c == 0)
    def _():
        state_ref[...] = jnp.zeros_like(state_ref)

    x = x_ref[...]
    h = _rmsnorm(x, prew_ref[...]).astype(BF16)

    def project(c0, c1):
        proj_ref[:, c0:c1] = _dot(h, win_ref[:, c0:c1])

    project(FORGET_DIM, 2 * FORGET_DIM)
    project(0, FORGET_DIM)
    project(2 * FORGET_DIM, 2 * FORGET_DIM + REC_WIDTH)

    def forget_gates(hd):
        off = hd * LANES
        f_raw = proj_ref[:, FORGET_DIM + off:FORGET_DIM + off + LANES]

        logits = lbl_ref[:, off:off + LANES]
        ex = jnp.exp(logits - jnp.max(logits, axis=0, keepdims=True))
        lb = jnp.sum(ex[1:layer + 1], axis=0, keepdims=True) / jnp.sum(ex, axis=0, keepdims=True)

        e = jnp.exp(-jnp.abs(f_raw))
        sp = 1.0 / (1.0 + e)
        pos = f_raw >= 0
        sig = jnp.where(pos, sp, e * sp)
        nsig = jnp.where(pos, e * sp, sp)
        log2_f = jnp.log2(lb + (1.0 - lb) * sig)
        k_ref[:, off:off + LANES] = ((1.0 - lb) * nsig).astype(BF16)
        g_hi = log2_f.astype(BF16)
        ghi_ref[:, off:off + LANES] = g_hi
        glo_ref[:, off:off + LANES] = (log2_f - g_hi.astype(F32)).astype(BF16)

    for hd in range(REC_HEADS):
        forget_gates(hd)

    o_ref[...] = xkeep_ref[...] + _rmsnorm(y_ref[...], postw_ref[...])
    xkeep_ref[...] = x

    bsum = bsum_ref[...]
    spread = -jnp.min(_dot(bsum, ghi_ref[...]) + _dot(bsum, glo_ref[...]))

    ltri = ltri_ref[...]
    b_all = _dot(ltri, ghi_ref[...]) + _dot(ltri, glo_ref[...])
    for hd in range(REC_HEADS):
        b_ref[hd] = b_all[:, hd * LANES:(hd + 1) * LANES]
        q_raw = proj_ref[:, hd * LANES:(hd + 1) * LANES]
        q_ref[:, hd * LANES:(hd + 1) * LANES] = (q_raw * _sigmoid(q_raw)).astype(BF16)

    def head_scores(hd, bounded):
        off = hd * LANES
        b = b_ref[hd]
        q = q_ref[:, off:off + LANES]
        k = k_ref[:, off:off + LANES]
        lv = lv_ref[...]

        seg_scores = []
        for s0 in range(0, chunk, seg):
            rows = slice(s0, s0 + seg)
            qs, ks, bs = q[rows], k[rows], b[rows]
            if bounded:
                b3 = bs.reshape(seg // REC_BASE, REC_BASE, LANES)
                mid = 0.5 * (b3[:, 0:1, :] + b3[:, REC_BASE - 1:REC_BASE, :])
                r = jnp.broadcast_to(mid, b3.shape).reshape(seg, LANES)
                a = jnp.where(lv < base_levels,
                              _dot(qs * jnp.exp2(bs - r).astype(BF16),
                                   jnp.transpose(ks.astype(F32) * jnp.exp2(r - bs)).astype(BF16)), 0.0)
                first_level = base_levels
            else:
                a = jnp.where(lv == -1, _dot_nt(qs, ks), 0.0)
                first_level = 0
            for j in range(first_level, seg_levels):
                en = jnp.exp2(-jnp.abs(bs - _boundary_rows(bs, 1 << j))).astype(BF16)
                a = jnp.where(lv == j, _dot_nt(qs * en, ks * en), a)
            seg_scores.append(a.astype(BF16))
        block_scores = []
        m = seg
        while m < chunk:
            for base in range(0, chunk, 2 * m):
                lo, hi = slice(base, base + m), slice(base + m, base + 2 * m)
                r = b[base + m - 1:base + m]
                qh = q[hi] * jnp.exp2(b[hi] - r).astype(BF16)
                kh_t = jnp.transpose(k[lo].astype(F32) * jnp.exp2(r - b[lo])).astype(BF16)
                block_scores.append((base, m, _dot(qh, kh_t).astype(BF16)))
            m *= 2
        b_last = b[chunk - 1:chunk, :]
        st = state_ref[hd]
        o_inter = _dot(q * jnp.exp2(b).astype(BF16), jnp.transpose(st).astype(BF16))
        kd = k * jnp.exp2(b_last - b).astype(BF16)
        return seg_scores, block_scores, o_inter, kd, st * jnp.exp2(b_last)

    def head_finish(hd, scores):
        seg_scores, block_scores, o_inter, kd, st_decayed = scores
        off = 2 * FORGET_DIM + hd * LANES
        v = proj_ref[:, off:off + LANES].astype(BF16)
        outs = [_dot(a, v[i * seg:(i + 1) * seg]) for i, a in enumerate(seg_scores)]
        for base, m, a in block_scores:
            upd = _dot(a, v[base:base + m])
            for i in range(m // seg):
                outs[(base + m) // seg + i] += upd[i * seg:(i + 1) * seg]
        o = jnp.concatenate(outs, axis=0) + o_inter
        state_ref[hd] = st_decayed + _dot_tn(v, kd)
        return o * lax.rsqrt(jnp.mean(o * o, axis=-1, keepdims=True) + NORM_EPS) * gnw_ref[...]

    def mix_and_output(bounded):
        z0 = 2 * FORGET_DIM + REC_WIDTH
        z_chunks = [(z0 + i * REC_WIDTH // 4, z0 + (i + 1) * REC_WIDTH // 4) for i in range(4)]
        heads_per_chunk = REC_HEADS // len(z_chunks)
        pending = []

        def gate(upto_chunk):
            while pending and pending[0][0] // heads_per_chunk < upto_chunk:
                hd, o = pending.pop(0)
                z = proj_ref[:, z0 + hd * LANES:z0 + (hd + 1) * LANES]
                gated_ref[:, hd * LANES:(hd + 1) * LANES] = (o * (z * _sigmoid(z))).astype(BF16)

        n_gate_chunks = len(z_chunks)
        ahead = 2
        queue = [head_scores(i, bounded) for i in range(ahead)]
        for hd in range(REC_HEADS):
            scores = queue.pop(0)
            if hd + ahead < REC_HEADS:
                queue.append(head_scores(hd + ahead, bounded))
            pending.append((hd, head_finish(hd, scores)))
            gate(min(hd, n_gate_chunks))
            if z_chunks:
                project(*z_chunks.pop(0))
        gate(n_gate_chunks)
        y_ref[...] = _dot(gated_ref[...], wout_ref[...])

    lax.cond(0.5 * spread <= SAFE_LOG2_EXPONENT, lambda: mix_and_output(True), lambda: mix_and_output(False))


def _level_index(c):
    t = np.arange(c)[:, None]
    s = np.arange(c)[None, :]
    lv = np.floor(np.log2(np.maximum(t ^ s, 1))).astype(np.int32)
    lv = np.where(s < t, lv, np.where(s == t, -1, 99))
    return jnp.asarray(lv, dtype=jnp.int32)


def _rec_layer(x, layer, lb_logits, pre_w, post_w, w_in, gnorm_w, w_out):
    B, T, D = x.shape
    chunk = min(REC_CHUNK, T)
    seg = min(REC_SEG, chunk)
    assert T % chunk == 0 and chunk & (chunk - 1) == 0 and seg >= 2 * SUBLANES and D == D_MODEL
    assert seg % REC_BASE == 0 and chunk % seg == 0
    depth = lb_logits.shape[0]
    ltri = jnp.asarray(np.tril(np.ones((chunk, chunk), np.float32)), dtype=BF16)
    bsum_np = np.zeros((max(SUBLANES, chunk // REC_BASE), chunk), np.float32)
    for i in range(chunk // REC_BASE):
        bsum_np[i, i * REC_BASE + 1:(i + 1) * REC_BASE] = 1.0
    bsum = jnp.asarray(bsum_np, dtype=BF16)
    const = lambda s: (0, 0)
    n_chunks, cur, prev = _staggered_tiles(B, T // chunk)
    return pl.pallas_call(
        functools.partial(_rec_layer_kernel, layer, T // chunk, n_chunks),
        out_shape=jax.ShapeDtypeStruct((B, T, D), x.dtype),
        grid=(n_chunks + 1,),
        in_specs=[
            pl.BlockSpec((None, chunk, D), lambda s: (*cur(s), 0)),
            pl.BlockSpec((depth, FORGET_DIM), const),
            pl.BlockSpec((seg, seg), const),
            pl.BlockSpec((chunk, chunk), const),
            pl.BlockSpec(bsum.shape, const),
            pl.BlockSpec((1, D), const),
            pl.BlockSpec((1, D), const),
            pl.BlockSpec((D, REC_IN), const),
            pl.BlockSpec((1, REC_VALUE_DIM), const),
            pl.BlockSpec((REC_WIDTH, D), const),
        ],
        out_specs=pl.BlockSpec((None, chunk, D), lambda s: (*prev(s), 0)),
        scratch_shapes=[
            pltpu.VMEM((REC_HEADS, REC_VALUE_DIM, REC_KEY_DIM), F32),
            pltpu.VMEM((chunk, REC_IN), F32),
            pltpu.VMEM((chunk, FORGET_DIM), BF16),
            pltpu.VMEM((chunk, FORGET_DIM), BF16),
            pltpu.VMEM((chunk, FORGET_DIM), BF16),
            pltpu.VMEM((chunk, FORGET_DIM), BF16),
            pltpu.VMEM((REC_HEADS, chunk, REC_KEY_DIM), F32),
            pltpu.VMEM((chunk, REC_WIDTH), BF16),
            pltpu.VMEM((chunk, D), F32),
            pltpu.VMEM((chunk, D), F32),
        ],
        compiler_params=pltpu.CompilerParams(
            dimension_semantics=("arbitrary",),
            vmem_limit_bytes=VMEM_LIMIT_BYTES),
        name="hgrn2_layer",
    )(x, lb_logits.astype(F32), _level_index(seg), ltri, bsum, pre_w.reshape(1, D), post_w.reshape(1, D),
      w_in.astype(BF16), gnorm_w.reshape(1, REC_VALUE_DIM), w_out.astype(BF16))


def kernel(x, positions, pre_norm_w, post_norm_w, attn_w_in, attn_b_in, attn_sinks, attn_w_out, attn_b_out,
           rec_w_in, rec_lb_logits, rec_gnorm_w, rec_w_out):
    depth = pre_norm_w.shape[0]
    for layer in range(depth):
        j = layer // N_MIXERS
        if layer % N_MIXERS == 0:
            x = _attn_layer(x, positions, pre_norm_w[layer], post_norm_w[layer], attn_w_in[j], attn_b_in[j],
                            attn_sinks[j], attn_w_out[j], attn_b_out[j])
        else:
            x = _rec_layer(x, layer, rec_lb_logits, pre_norm_w[layer], post_norm_w[layer], rec_w_in[j],
                           rec_gnorm_w[j], rec_w_out[j])
    return x
```
